```python
import math
import jax, jax.numpy as jnp
from jax import lax
import numpy as np

D_MODEL = 1024
BATCH = 32
SEQ = 2048
DEPTH = 1
DEC_BATCH = 128
DEC_SEQ = 4
PAST_LEN = 8192
PAGE_SIZE = 128

HEAD_DIM = 64
A_HEADS = D_MODEL // 256
B_HEADS = D_MODEL // 128
A_QK = A_HEADS * 2 * HEAD_DIM
A_V = A_HEADS * 2 * HEAD_DIM
B_QK = B_HEADS * HEAD_DIM
B_V = B_HEADS * HEAD_DIM
IN_SPLITS = (A_QK, A_QK, A_V, B_QK, B_QK, B_V, B_HEADS, 2 * D_MODEL)
IN_OFFSETS = tuple(int(v) for v in np.cumsum(IN_SPLITS)[:-1])
IN_WIDTH = sum(IN_SPLITS)
ROT_DIM = HEAD_DIM // 4
ROPE_THETA = 500000.0
ATTN_SCALE = HEAD_DIM ** -0.5
Q_BLOCK = 128
N_EXPERTS = 32
TOP_K = 4
D_FF = D_MODEL
SWIGLU_LIMIT = 7.0
SWIGLU_ALPHA = 1.702
MOE_BLOCK = 128
NORM_EPS = 1e-6
SUBLN_EPS = 1e-5
FORGET_BIAS_INIT = 4.0

kernel_name = "diff_fox_gated_moe_adaln_step"


def rmsnorm(x, g, eps=NORM_EPS):
    xf = x.astype(jnp.float32)
    y = xf * lax.rsqrt(jnp.mean(xf * xf, axis=-1, keepdims=True) + eps)
    return (y * g.astype(jnp.float32)).astype(x.dtype)


def adaln_params(c, w_mod, b_mod):
    mod = jax.nn.silu(c) @ w_mod + b_mod
    return mod.reshape(c.shape[0], 6, D_MODEL)


def modulate(h, shift, scale):
    return h * (1.0 + scale[:, None, :]) + shift[:, None, :]


def partial_rope(x, pos):
    half = ROT_DIM // 2
    inv_freq = jnp.power(ROPE_THETA, -jnp.arange(half, dtype=jnp.float32) * 2.0 / ROT_DIM)
    ang = pos[:, None] * inv_freq[None, :]
    cos = jnp.cos(ang)[None, :, None, None, :].astype(x.dtype)
    sin = jnp.sin(ang)[None, :, None, None, :].astype(x.dtype)
    x1, x2, rest = x[..., :half], x[..., half:ROT_DIM], x[..., ROT_DIM:]
    return jnp.concatenate([x1 * cos - x2 * sin, x2 * cos + x1 * sin, rest], axis=-1)


def project_mixers(h, pos, w_in, b_forget, b_gate):
    bsz, seq = h.shape[:2]
    proj = jnp.einsum('bsd,de->bse', h, w_in)
    qa, ka, va, qb, kb, vb, f_logit, g_logit = jnp.split(proj, IN_OFFSETS, axis=-1)
    qa = partial_rope(qa.reshape(bsz, seq, A_HEADS, 2, HEAD_DIM), pos)
    ka = partial_rope(ka.reshape(bsz, seq, A_HEADS, 2, HEAD_DIM), pos)
    va = va.reshape(bsz, seq, A_HEADS, 2 * HEAD_DIM)
    qb = qb.reshape(bsz, seq, B_HEADS, HEAD_DIM)
    kb = kb.reshape(bsz, seq, B_HEADS, HEAD_DIM)
    vb = vb.reshape(bsz, seq, B_HEADS, HEAD_DIM)
    logf = jax.nn.log_sigmoid(f_logit.astype(jnp.float32) + b_forget.astype(jnp.float32))
    gates = jax.nn.sigmoid(g_logit + b_gate)
    return qa, ka, va, qb, kb, vb, logf, gates[..., :D_MODEL], gates[..., D_MODEL:]


def diff_attn_prompt(qa, ka, va, lam):
    bsz, seq = qa.shape[:2]
    nb = seq // Q_BLOCK
    qs = (qa * ATTN_SCALE).reshape(bsz, nb, Q_BLOCK, A_HEADS, 2, HEAD_DIM).swapaxes(0, 1)
    key_pos = jnp.arange(seq)

    def block(args):
        q_blk, i = args
        s = jnp.einsum('bqhmd,bkhmd->bhmqk', q_blk, ka, preferred_element_type=jnp.float32)
        q_pos = i * Q_BLOCK + jnp.arange(Q_BLOCK)
        s = jnp.where(key_pos[None, :] <= q_pos[:, None], s, -jnp.inf)
        p = jax.nn.softmax(s, axis=-1)
        o = jnp.einsum('bhmqk,bkhe->bqhme', p, va, preferred_element_type=jnp.float32)
        return (o[..., 0, :] - lam * o[..., 1, :]).astype(va.dtype)

    o = lax.map(block, (qs, jnp.arange(nb)))
    return o.swapaxes(0, 1).reshape(bsz, seq, A_HEADS, 2 * HEAD_DIM)


def fox_prompt(qb, kb, vb, logf):
    bsz, seq = qb.shape[:2]
    nb = seq // Q_BLOCK
    cum = jnp.cumsum(logf, axis=1)
    cum_k = cum.transpose(0, 2, 1)
    qs = (qb * ATTN_SCALE).reshape(bsz, nb, Q_BLOCK, B_HEADS, HEAD_DIM).swapaxes(0, 1)
    cq = cum.reshape(bsz, nb, Q_BLOCK, B_HEADS).swapaxes(0, 1)
    key_pos = jnp.arange(seq)

    def block(args):
        q_blk, c_blk, i = args
        s = jnp.einsum('bqhd,bkhd->bhqk', q_blk, kb, preferred_element_type=jnp.float32)
        s = s + c_blk.transpose(0, 2, 1)[..., None] - cum_k[:, :, None, :]
        q_pos = i * Q_BLOCK + jnp.arange(Q_BLOCK)
        s = jnp.where(key_pos[None, :] <= q_pos[:, None], s, -jnp.inf)
        p = jax.nn.softmax(s, axis=-1)
        return jnp.einsum('bhqk,bkhd->bqhd', p, vb, preferred_element_type=jnp.float32).astype(vb.dtype)

    o = lax.map(block, (qs, cq, jnp.arange(nb)))
    return o.swapaxes(0, 1).reshape(bsz, seq, B_HEADS, HEAD_DIM)


def diff_attn_sample(qa, ka, va, cache_k, cache_v, page_table, layer, lam):
    db, nq = qa.shape[:2]
    n_pages, page_size = page_table.shape[1], cache_k.shape[2]
    past_len = n_pages * page_size
    qs = qa * ATTN_SCALE

    def page_scores(p):
        k = cache_k[layer, page_table[:, p]]
        return jnp.einsum('bqhmd,bkhmd->bhmqk', qs, k, preferred_element_type=jnp.float32)

    s_past = jnp.moveaxis(lax.map(page_scores, jnp.arange(n_pages)), 0, 4).reshape(db, A_HEADS, 2, nq, past_len)
    s_new = jnp.einsum('bqhmd,bkhmd->bhmqk', qs, ka, preferred_element_type=jnp.float32)
    s_new = jnp.where(jnp.tril(jnp.ones((nq, nq), dtype=bool)), s_new, -jnp.inf)
    p = jax.nn.softmax(jnp.concatenate([s_past, s_new], axis=-1), axis=-1)
    p_pages = jnp.moveaxis(p[..., :past_len].reshape(db, A_HEADS, 2, nq, n_pages, page_size), 4, 0)
    o_new = jnp.einsum('bhmqk,bkhe->bqhme', p[..., past_len:], va, preferred_element_type=jnp.float32)

    def add_page(acc, xs):
        idx, p_pg = xs
        v = cache_v[layer, page_table[:, idx]]
        return acc + jnp.einsum('bhmqk,bkhe->bqhme', p_pg, v, preferred_element_type=jnp.float32), None

    o, _ = lax.scan(add_page, o_new, (jnp.arange(n_pages), p_pages))
    return (o[..., 0, :] - lam * o[..., 1, :]).astype(va.dtype)


def fox_sample(qb, kb, vb, logf, cache_k, cache_v, cache_logf, page_table, layer):
    db, nq = qb.shape[:2]
    n_pages, page_size = page_table.shape[1], cache_k.shape[2]
    past_len = n_pages * page_size
    qs = qb * ATTN_SCALE
    logf_past = cache_logf[layer, page_table].reshape(db, past_len, B_HEADS).astype(jnp.float32)
    tail = lax.cumsum(logf_past, axis=1, reverse=True) - logf_past
    cum_new = jnp.cumsum(logf, axis=1).transpose(0, 2, 1)

    def page_scores(p):
        k = cache_k[layer, page_table[:, p]]
        return jnp.einsum('bqhd,bkhd->bhqk', qs, k, preferred_element_type=jnp.float32)

    s_past = jnp.moveaxis(lax.map(page_scores, jnp.arange(n_pages)), 0, 3).reshape(db, B_HEADS, nq, past_len)
    s_past = s_past + cum_new[..., None] + tail.transpose(0, 2, 1)[:, :, None, :]
    s_new = jnp.einsum('bqhd,bkhd->bhqk', qs, kb, preferred_element_type=jnp.float32)
    s_new = s_new + cum_new[..., :, None] - cum_new[..., None, :]
    s_new = jnp.where(jnp.tril(jnp.ones((nq, nq), dtype=bool)), s_new, -jnp.inf)
    p = jax.nn.softmax(jnp.concatenate([s_past, s_new], axis=-1), axis=-1)
    p_pages = jnp.moveaxis(p[..., :past_len].reshape(db, B_HEADS, nq, n_pages, page_size), 3, 0)
    o_new = jnp.einsum('bhqk,bkhd->bqhd', p[..., past_len:], vb, preferred_element_type=jnp.float32)

    def add_page(acc, xs):
        idx, p_pg = xs
        v = cache_v[layer, page_table[:, idx]]
        return acc + jnp.einsum('bhqk,bkhd->bqhd', p_pg, v, preferred_element_type=jnp.float32), None

    o, _ = lax.scan(add_page, o_new, (jnp.arange(n_pages), p_pages))
    return o.astype(vb.dtype)


def merge_branches(oa, ob, gate_a, gate_b, g_subln, lambda_init, w_br_a, w_br_b, w_out):
    bsz, seq = oa.shape[:2]
    oa = rmsnorm(oa, g_subln, SUBLN_EPS) * (1.0 - lambda_init)
    y = gate_a * (oa.reshape(bsz, seq, A_V) @ w_br_a) + gate_b * (ob.reshape(bsz, seq, B_V) @ w_br_b)
    return y @ w_out


def expert_swiglu(xb, w1, b1, w2, b2):
    hh = xb @ w1 + b1
    x_glu = jnp.minimum(hh[:, :D_FF], SWIGLU_LIMIT)
    x_lin = jnp.clip(hh[:, D_FF:], -SWIGLU_LIMIT, SWIGLU_LIMIT)
    act = x_glu * jax.nn.sigmoid(SWIGLU_ALPHA * x_glu) * (x_lin + 1.0)
    return act @ w2 + b2


def moe(h, w_router, b_router, w1, b1, w2, b2):
    lead = h.shape[:-1]
    xt = h.reshape(-1, D_MODEL)
    n_tok = xt.shape[0]
    logits = jnp.einsum('nd,de->ne', xt, w_router, preferred_element_type=jnp.float32) + b_router.astype(jnp.float32)
    top_val, top_idx = lax.top_k(logits, TOP_K)
    weight = jax.nn.softmax(top_val, axis=-1)
    n_assign = n_tok * TOP_K
    expert_flat = top_idx.reshape(-1)
    order = jnp.argsort(expert_flat)
    exp_sorted = expert_flat[order]
    tok_sorted = order // TOP_K
    counts = jnp.bincount(expert_flat, length=N_EXPERTS)
    padded = (counts + MOE_BLOCK - 1) // MOE_BLOCK * MOE_BLOCK
    start = jnp.cumsum(counts) - counts
    pad_end = jnp.cumsum(padded)
    dest = (pad_end - padded)[exp_sorted] + jnp.arange(n_assign) - start[exp_sorted]
    n_blocks = -(-n_assign // MOE_BLOCK) + N_EXPERTS
    x_disp = jnp.zeros((n_blocks * MOE_BLOCK, D_MODEL), xt.dtype).at[dest].set(xt[tok_sorted])
    block_expert = jnp.minimum(jnp.searchsorted(pad_end, jnp.arange(n_blocks) * MOE_BLOCK, side='right'), N_EXPERTS - 1)

    def run_block(args):
        xb, e = args
        return expert_swiglu(xb, w1[e], b1[e], w2[e], b2[e])

    y_disp = lax.map(run_block, (x_disp.reshape(n_blocks, MOE_BLOCK, D_MODEL), block_expert))
    y = y_disp.reshape(-1, D_MODEL)[dest].astype(jnp.float32) * weight.reshape(-1)[order][:, None]
    out = jax.ops.segment_sum(y, tok_sorted, num_segments=n_tok)
    return out.astype(h.dtype).reshape(*lead, D_MODEL)


def ffn_sublayer(x, mod, g_pre, g_post, w_router, b_router, w1, b1, w2, b2):
    h = modulate(rmsnorm(x, g_pre), mod[:, 3], mod[:, 4])
    y = moe(h, w_router, b_router, w1, b1, w2, b2)
    return x + mod[:, 5, None] * rmsnorm(y, g_post)


def setup_inputs(seed: int = 0) -> dict:
    key = jax.random.key(seed)
    keys = iter(jax.random.split(key, 64))

    def nrm(shape, scale=1.0):
        return scale * jax.random.normal(next(keys), shape, dtype=jnp.float32)

    n_pages = PAST_LEN // PAGE_SIZE
    n_used = DEC_BATCH * n_pages
    n_pool = n_used + max(1, n_used // 4)
    page_table = jax.random.permutation(next(keys), n_pool)[:n_used].reshape(DEC_BATCH, n_pages).astype(jnp.int32)
    pool = (DEPTH, n_pool, PAGE_SIZE)
    L = (DEPTH,)
    return {
        'x_prompt': nrm((BATCH, SEQ, D_MODEL)),
        'x_sample': nrm((DEC_BATCH, DEC_SEQ, D_MODEL)),
        'c_prompt': nrm((BATCH, D_MODEL)),
        'c_sample': nrm((DEC_BATCH, D_MODEL)),
        'cache_a_k': nrm(pool + (A_HEADS, 2, HEAD_DIM)),
        'cache_a_v': nrm(pool + (A_HEADS, 2 * HEAD_DIM)),
        'cache_b_k': nrm(pool + (B_HEADS, HEAD_DIM)),
        'cache_b_v': nrm(pool + (B_HEADS, HEAD_DIM)),
        'cache_b_logf': jax.nn.log_sigmoid(FORGET_BIAS_INIT + nrm(pool + (B_HEADS,))),
        'page_table': page_table,
        'w_mod': nrm(L + (D_MODEL, 6 * D_MODEL), 0.5 * D_MODEL ** -0.5),
        'b_mod': nrm(L + (6 * D_MODEL,), 0.02),
        'g_pre_mix': 1.0 + nrm(L + (D_MODEL,), 0.02),
        'g_post_mix': 1.0 + nrm(L + (D_MODEL,), 0.02),
        'g_pre_ffn': 1.0 + nrm(L + (D_MODEL,), 0.02),
        'g_post_ffn': 1.0 + nrm(L + (D_MODEL,), 0.02),
        'w_in': nrm(L + (D_MODEL, IN_WIDTH), D_MODEL ** -0.5),
        'b_forget': FORGET_BIAS_INIT + nrm(L + (B_HEADS,), 0.1),
        'b_gate': nrm(L + (2 * D_MODEL,), 0.02),
        'lambda_q1': nrm(L + (HEAD_DIM,), 0.1),
        'lambda_k1': nrm(L + (HEAD_DIM,), 0.1),
        'lambda_q2': nrm(L + (HEAD_DIM,), 0.1),
        'lambda_k2': nrm(L + (HEAD_DIM,), 0.1),
        'g_subln': 1.0 + nrm(L + (2 * HEAD_DIM,), 0.02),
        'w_br_a': nrm(L + (A_V, D_MODEL), A_V ** -0.5),
        'w_br_b': nrm(L + (B_V, D_MODEL), B_V ** -0.5),
        'w_out': nrm(L + (D_MODEL, D_MODEL), D_MODEL ** -0.5),
        'w_router': nrm(L + (D_MODEL, N_EXPERTS), D_MODEL ** -0.5),
        'b_router': nrm(L + (N_EXPERTS,), 0.01),
        'w_mlp1': nrm(L + (N_EXPERTS, D_MODEL, 2 * D_FF), D_MODEL ** -0.5),
        'b_mlp1': nrm(L + (N_EXPERTS, 2 * D_FF), 0.02),
        'w_mlp2': nrm(L + (N_EXPERTS, D_FF, D_MODEL), D_FF ** -0.5),
        'b_mlp2': nrm(L + (N_EXPERTS, D_MODEL), 0.02),
    }


def reference(x_prompt, x_sample, c_prompt, c_sample, cache_a_k, cache_a_v, cache_b_k, cache_b_v, cache_b_logf,
              page_table, w_mod, b_mod, g_pre_mix, g_post_mix, g_pre_ffn, g_post_ffn, w_in, b_forget, b_gate,
              lambda_q1, lambda_k1, lambda_q2, lambda_k2, g_subln, w_br_a, w_br_b, w_out,
              w_router, b_router, w_mlp1, b_mlp1, w_mlp2, b_mlp2):
    past_len = page_table.shape[1] * cache_a_k.shape[2]
    pos_p = jnp.arange(x_prompt.shape[1], dtype=jnp.float32)
    pos_s = past_len + jnp.arange(x_sample.shape[1], dtype=jnp.float32)
    xp, xs = x_prompt, x_sample
    rows_p, rows_s = [], []
    for l in range(DEPTH):
        lam_init = 0.8 - 0.6 * math.exp(-0.3 * l)
        lam = (jnp.exp(jnp.sum(lambda_q1[l] * lambda_k1[l])) - jnp.exp(jnp.sum(lambda_q2[l] * lambda_k2[l]))
               + lam_init).astype(jnp.float32)
        mod_p = adaln_params(c_prompt, w_mod[l], b_mod[l])
        mod_s = adaln_params(c_sample, w_mod[l], b_mod[l])

        qa, ka, va, qb, kb, vb, lf, ga, gb = project_mixers(
            modulate(rmsnorm(xp, g_pre_mix[l]), mod_p[:, 0], mod_p[:, 1]), pos_p, w_in[l], b_forget[l], b_gate[l])
        y = merge_branches(diff_attn_prompt(qa, ka, va, lam), fox_prompt(qb, kb, vb, lf), ga, gb,
                           g_subln[l], lam_init, w_br_a[l], w_br_b[l], w_out[l])
        xp = xp + mod_p[:, 2, None] * rmsnorm(y, g_post_mix[l])
        rows_p.append((ka, va, kb, vb, lf))

        qa, ka, va, qb, kb, vb, lf, ga, gb = project_mixers(
            modulate(rmsnorm(xs, g_pre_mix[l]), mod_s[:, 0], mod_s[:, 1]), pos_s, w_in[l], b_forget[l], b_gate[l])
        oa = diff_attn_sample(qa, ka, va, cache_a_k, cache_a_v, page_table, l, lam)
        ob = fox_sample(qb, kb, vb, lf, cache_b_k, cache_b_v, cache_b_logf, page_table, l)
        y = merge_branches(oa, ob, ga, gb, g_subln[l], lam_init, w_br_a[l], w_br_b[l], w_out[l])
        xs = xs + mod_s[:, 2, None] * rmsnorm(y, g_post_mix[l])
        rows_s.append((ka, va, kb, vb, lf))

        xp = ffn_sublayer(xp, mod_p, g_pre_ffn[l], g_post_ffn[l], w_router[l], b_router[l],
                          w_mlp1[l], b_mlp1[l], w_mlp2[l], b_mlp2[l])
        xs = ffn_sublayer(xs, mod_s, g_pre_ffn[l], g_post_ffn[l], w_router[l], b_router[l],
                          w_mlp1[l], b_mlp1[l], w_mlp2[l], b_mlp2[l])

    return (xp, xs,
            jnp.stack([r[0] for r in rows_p]), jnp.stack([r[1] for r in rows_p]),
            jnp.stack([r[2] for r in rows_p]), jnp.stack([r[3] for r in rows_p]),
            jnp.stack([r[4] for r in rows_p]),
            jnp.stack([r[0] for r in rows_s]), jnp.stack([r[1] for r in rows_s]),
            jnp.stack([r[2] for r in rows_s]), jnp.stack([r[3] for r in rows_s]),
            jnp.stack([r[4] for r in rows_s]))
```

```python
import functools

import jax
import jax.numpy as jnp
from jax import lax
from jax.experimental import pallas as pl
from jax.experimental.pallas import tpu as pltpu

F32 = jnp.float32
BF16 = jnp.bfloat16

HEAD_DIM = 64
ROT_DIM = HEAD_DIM // 4
ROPE_THETA = 500000.0
ATTN_SCALE = HEAD_DIM ** -0.5
N_EXPERTS = 32
TOP_K = 4
SWIGLU_LIMIT = 7.0
SWIGLU_ALPHA = 1.702
NORM_EPS = 1e-6
SUBLN_EPS = 1e-5
LAMBDA_INIT = 0.8 - 0.6

LANES = 128
QKV_WIDTH = 512
TOKEN_TILE = 256
ATTN_TILE = 256
EXPERT_TILE = 512
DECODE_PAGES = 8
VMEM_LIMIT = 56 * 1024 * 1024

NT_DIMS = (((1,), (1,)), ((), ()))


def _dot(a, b):
    return jnp.dot(a, b, preferred_element_type=F32)


def _split_bf16(x):
    hi = x.astype(BF16)
    lo = (x - hi.astype(F32)).astype(BF16)
    return hi, lo


def _rms(x, eps):
    return x * lax.rsqrt(jnp.mean(x * x, axis=-1, keepdims=True) + eps)


def _lam(lq1, lk1, lq2, lk2):
    a = jnp.exp(jnp.sum(lq1[...] * lk1[...], axis=1, keepdims=True))
    b = jnp.exp(jnp.sum(lq2[...] * lk2[...], axis=1, keepdims=True))
    return a - b + LAMBDA_INIT


def _mod_kernel(c_ref, w_ref, b_ref, o_ref):
    c = c_ref[...]
    s_hi, s_lo = _split_bf16(c * jax.nn.sigmoid(c))
    w_hi, w_lo = _split_bf16(w_ref[...])
    o_ref[...] = _dot(s_hi, w_hi) + _dot(s_lo, w_hi) + _dot(s_hi, w_lo) + b_ref[...]


def _adaln(c, w_mod, b_mod):
    n, d = c.shape
    width = w_mod.shape[1]
    return pl.pallas_call(
        _mod_kernel,
        grid=(width // d,),
        in_specs=[pl.BlockSpec((n, d), lambda j: (0, 0)),
                  pl.BlockSpec((d, d), lambda j: (0, j)),
                  pl.BlockSpec((1, d), lambda j: (0, j))],
        out_specs=pl.BlockSpec((n, d), lambda j: (0, j)),
        out_shape=jax.ShapeDtypeStruct((n, width), F32),
        name="adaln_mod",
    )(c, w_mod, b_mod.reshape(1, width))


def _proj_kernel(x_ref, shift_ref, scale_ref, g_ref, cos_ref, sa_ref, sb_ref,
                 wqkv_ref, wf_ref, wg_ref, bf_ref, bg_ref,
                 qa_ref, ka_ref, va_ref, qb_ref, kb_ref, vb_ref, lf_ref, gate_ref):
    x = x_ref[...]
    d = x.shape[-1]
    shift = shift_ref[...].reshape(-1, d)
    scale = scale_ref[...].reshape(-1, d)
    h = _rms(x, NORM_EPS) * g_ref[...]
    hb = (h * (1.0 + scale) + shift).astype(BF16)
    cos, sa, sb = cos_ref[...], sa_ref[...], sb_ref[...]

    def rope(y):
        parts = []
        for g in range(QKV_WIDTH // LANES):
            yg = y[:, g * LANES:(g + 1) * LANES]
            parts.append(yg * cos + pltpu.roll(yg, ROT_DIM // 2, 1) * sa
                         + pltpu.roll(yg, LANES - ROT_DIM // 2, 1) * sb)
        return jnp.concatenate(parts, axis=1)

    def seg(i):
        return _dot(hb, wqkv_ref[:, i * QKV_WIDTH:(i + 1) * QKV_WIDTH])

    qa_ref[...] = (rope(seg(0)) * ATTN_SCALE).astype(BF16)
    ka_ref[...] = rope(seg(1))
    va_ref[...] = seg(2)
    qb_ref[...] = (seg(3) * ATTN_SCALE).astype(BF16)
    kb_ref[...] = seg(4)
    vb_ref[...] = seg(5)
    z = (_dot(hb, wf_ref[...]) + bf_ref[...])[:, :lf_ref.shape[-1]]
    lf_ref[...] = jnp.minimum(z, 0.0) - jnp.log1p(jnp.exp(-jnp.abs(z)))
    gate_ref[...] = jax.nn.sigmoid(_dot(hb, wg_ref[...]) + bg_ref[...])


def _project(x2d, shift, scale, g_pre, tabs, wqkv, wf, wg, bf, bg, n_heads_b, rows_per_seq):
    n, d = x2d.shape
    tm = min(TOKEN_TILE, n)
    if shift.ndim == 3:
        tps = rows_per_seq // tm
        mod_spec = pl.BlockSpec((1, 1, d), lambda i: (i // tps, 0, 0))
        tab_spec = pl.BlockSpec((tm, LANES), lambda i: (i % tps, 0))
    else:
        mod_spec = pl.BlockSpec((tm, d), lambda i: (i, 0))
        tab_spec = pl.BlockSpec((tm, LANES), lambda i: (i, 0))
    const = lambda shape: pl.BlockSpec(shape, lambda i: (0, 0), pipeline_mode=pl.Buffered(1))
    row = lambda w: pl.BlockSpec((tm, w), lambda i: (i, 0))
    gw = wg.shape[1]
    return pl.pallas_call(
        _proj_kernel,
        grid=(n // tm,),
        in_specs=[row(d), mod_spec, mod_spec, const((1, d)), tab_spec, tab_spec, tab_spec,
                  const(wqkv.shape), const(wf.shape), const(wg.shape), const(bf.shape), const(bg.shape)],
        out_specs=[row(QKV_WIDTH)] * 6 + [row(n_heads_b), row(gw)],
        out_shape=[jax.ShapeDtypeStruct((n, QKV_WIDTH), BF16),
                   jax.ShapeDtypeStruct((n, QKV_WIDTH), F32),
                   jax.ShapeDtypeStruct((n, QKV_WIDTH), F32),
                   jax.ShapeDtypeStruct((n, QKV_WIDTH), BF16),
                   jax.ShapeDtypeStruct((n, QKV_WIDTH), F32),
                   jax.ShapeDtypeStruct((n, QKV_WIDTH), F32),
                   jax.ShapeDtypeStruct((n, n_heads_b), F32),
                   jax.ShapeDtypeStruct((n, gw), F32)],
        compiler_params=pltpu.CompilerParams(dimension_semantics=("arbitrary",), vmem_limit_bytes=VMEM_LIMIT),
        name="mixer_projection",
    )(x2d, shift, scale, g_pre, *tabs, wqkv, wf, wg, bf, bg)


def _rope_tables(pos):
    half = ROT_DIM // 2
    inv_freq = jnp.power(ROPE_THETA, -jnp.arange(half, dtype=F32) * 2.0 / ROT_DIM)
    ang = pos[:, None] * inv_freq[None, :]
    cos, sin = jnp.cos(ang), jnp.sin(ang)
    s = pos.shape[0]
    ones = jnp.ones((s, HEAD_DIM - ROT_DIM), F32)
    zeros = jnp.zeros((s, HEAD_DIM - ROT_DIM), F32)
    zh = jnp.zeros((s, half), F32)
    cos_t = jnp.concatenate([cos, cos, ones], axis=1)
    sa_t = jnp.concatenate([zh, sin, zeros], axis=1)
    sb_t = jnp.concatenate([-sin, zh, zeros], axis=1)
    rep = LANES // HEAD_DIM
    return tuple(jnp.tile(t, (1, rep)) for t in (cos_t, sa_t, sb_t))


def _cum_kernel(lf_ref, c_ref, ct_ref, *, chunk):
    s, nh = lf_ref.shape[1], lf_ref.shape[2]
    r = lax.broadcasted_iota(jnp.int32, (chunk, chunk), 0)
    c = lax.broadcasted_iota(jnp.int32, (chunk, chunk), 1)
    tri = (c <= r).astype(F32)
    carry = jnp.zeros((1, nh), F32)
    for i in range(s // chunk):
        x = lf_ref[0, i * chunk:(i + 1) * chunk, :]
        cum = jnp.dot(tri, x, preferred_element_type=F32, precision=lax.Precision.HIGHEST) + carry
        carry = cum[chunk - 1:chunk, :]
        cum_t = cum.T
        for p in range(nh // 2):
            c_ref[0, p, i * chunk:(i + 1) * chunk, :] = cum[:, 2 * p:2 * p + 2]
            ct_ref[0, p, :, i * chunk:(i + 1) * chunk] = cum_t[2 * p:2 * p + 2, :]


def _cum_logf(lf):
    b, s, nh = lf.shape
    chunk = min(256, s)
    return pl.pallas_call(
        functools.partial(_cum_kernel, chunk=chunk),
        grid=(b,),
        in_specs=[pl.BlockSpec((1, s, nh), lambda i: (i, 0, 0))],
        out_specs=[pl.BlockSpec((1, nh // 2, s, 2), lambda i: (i, 0, 0, 0)),
                   pl.BlockSpec((1, nh // 2, 2, s), lambda i: (i, 0, 0, 0))],
        out_shape=[jax.ShapeDtypeStruct((b, nh // 2, s, 2), F32),
                   jax.ShapeDtypeStruct((b, nh // 2, 2, s), F32)],
        name="forget_cumsum",
    )(lf)


def _pattn_kernel(*refs, t, mode):
    if mode == "diff":
        q_ref, k_ref, v_ref, lq1, lk1, lq2, lk2, gs_ref, o_ref, kt_s, v_s, m_s, l_s, acc_s = refs
    else:
        q_ref, k_ref, v_ref, c_ref, ct_ref, o_ref, kt_s, v_s, m_s, l_s, acc_s = refs
    qi = pl.program_id(2)

    @pl.when(qi == 0)
    def _():
        kt_s[...] = k_ref[0].T.astype(BF16)
        v_s[...] = v_ref[0].astype(BF16)

    q = q_ref[0]
    lane = lax.broadcasted_iota(jnp.int32, q.shape, 1)
    zero = jnp.zeros_like(q)
    qz = (jnp.where(lane < HEAD_DIM, q, zero), jnp.where(lane >= HEAD_DIM, q, zero))
    m_s[...] = jnp.full(m_s.shape, -jnp.inf, F32)
    l_s[...] = jnp.zeros(l_s.shape, F32)
    acc_s[...] = jnp.zeros(acc_s.shape, F32)
    if mode == "fox":
        cb = c_ref[0, 0]
        rowb = (cb[:, 0:1], cb[:, 1:2])
    row = lax.broadcasted_iota(jnp.int32, (t, t), 0)
    col = lax.broadcasted_iota(jnp.int32, (t, t), 1)

    def tile(j, masked):
        off = pl.multiple_of(j * t, t)
        kt = kt_s[:, pl.ds(off, t)]
        vt = v_s[pl.ds(off, t), :]
        for hf in range(2):
            s = _dot(qz[hf], kt)
            if mode == "fox":
                s = s + (rowb[hf] - ct_ref[0, 0, hf:hf + 1, pl.ds(off, t)])
            if masked:
                s = jnp.where(col <= row, s, -jnp.inf)
            m_old = m_s[hf]
            m_new = jnp.maximum(m_old, jnp.max(s, axis=1, keepdims=True))
            alpha = jnp.exp(m_old - m_new)
            p = jnp.exp(s - m_new)
            l_s[hf] = alpha * l_s[hf] + jnp.sum(p, axis=1, keepdims=True)
            acc_s[hf] = alpha * acc_s[hf] + _dot(p.astype(BF16), vt)
            m_s[hf] = m_new

    def body(j, carry):
        tile(j, False)
        return carry

    lax.fori_loop(0, qi, body, 0)
    tile(qi, True)

    o0 = acc_s[0] / l_s[0]
    o1 = acc_s[1] / l_s[1]
    if mode == "diff":
        o = o0 - _lam(lq1, lk1, lq2, lk2) * o1
        o = _rms(o, SUBLN_EPS) * gs_ref[...] * (1.0 - LAMBDA_INIT)
    else:
        o = jnp.where(lane < HEAD_DIM, o0, o1)
    o_ref[0] = o.astype(BF16)


def _prompt_attention(q, k, v, extra, mode):
    b, s, w = q.shape
    t = min(ATTN_TILE, s)
    nblk = w // LANES
    qspec = pl.BlockSpec((1, t, LANES), lambda bi, h, qi: (bi, qi, h))
    kvspec = pl.BlockSpec((1, s, LANES), lambda bi, h, qi: (bi, 0, h))
    if mode == "diff":
        small = lambda a: pl.BlockSpec(a.shape, lambda bi, h, qi: (0, 0))
        extra_specs = [small(a) for a in extra]
    else:
        extra_specs = [pl.BlockSpec((1, 1, t, 2), lambda bi, h, qi: (bi, h, qi, 0)),
                       pl.BlockSpec((1, 1, 2, s), lambda bi, h, qi: (bi, h, 0, 0))]
    return pl.pallas_call(
        functools.partial(_pattn_kernel, t=t, mode=mode),
        grid=(b, nblk, s // t),
        in_specs=[qspec, kvspec, kvspec] + extra_specs,
        out_specs=qspec,
        out_shape=jax.ShapeDtypeStruct((b, s, w), BF16),
        scratch_shapes=[pltpu.VMEM((LANES, s), BF16), pltpu.VMEM((s, LANES), BF16),
                        pltpu.VMEM((2, t, 1), F32), pltpu.VMEM((2, t, 1), F32), pltpu.VMEM((2, t, LANES), F32)],
        compiler_params=pltpu.CompilerParams(dimension_semantics=("arbitrary", "arbitrary", "arbitrary"),
                                             vmem_limit_bytes=VMEM_LIMIT),
        name="prompt_attention_" + mode,
    )(q, k, v, *extra)


def _decode_kernel(pt_ref, *refs, n_grp, mode):
    del pt_ref
    q_ref, kn_ref, vn_ref = refs[:3]
    pos = 3
    if mode == "diff":
        lq1, lk1, lq2, lk2, gs_ref = refs[pos:pos + 5]
        pos += 5
    else:
        lfn_ref = refs[pos]
        pos += 1
    kp = refs[pos:pos + n_grp]
    vp = refs[pos + n_grp:pos + 2 * n_grp]
    pos += 2 * n_grp
    if mode == "fox":
        lfp = refs[pos:pos + n_grp]
        pos += n_grp
    o_ref, m_s, l_s, acc_s = refs[pos:pos + 4]
    if mode == "fox":
        carry_s = refs[pos + 4]
    j = pl.program_id(1)
    nq, w = q_ref.shape[1], q_ref.shape[2]
    nchunk = w // HEAD_DIM
    rows = nq * nchunk
    page = kp[0].shape[1]

    def rep_rows(x):
        return jnp.concatenate([jnp.broadcast_to(x[i:i + 1], (nchunk, x.shape[1])) for i in range(nq)], axis=0)

    rid = lax.broadcasted_iota(jnp.int32, (rows, w), 0)
    lid = lax.broadcasted_iota(jnp.int32, (rows, w), 1)
    qbd = jnp.where(lid // HEAD_DIM == rid % nchunk, rep_rows(q_ref[0].astype(F32)), 0.0)
    qbd_b = qbd.astype(BF16)
    if mode == "fox":
        lfn = lfn_ref[0]
        cn = [lfn[0:1]]
        for i in range(1, nq):
            cn.append(cn[-1] + lfn[i:i + 1])
        r8 = lax.broadcasted_iota(jnp.int32, (rows, nchunk), 0)
        l8 = lax.broadcasted_iota(jnp.int32, (rows, nchunk), 1)
        sel = l8 == r8 % nchunk
        pick = lambda x: jnp.sum(jnp.where(sel, x, 0.0), axis=1, keepdims=True)
        rowb = pick(jnp.concatenate([jnp.broadcast_to(c, (nchunk, nchunk)) for c in cn], axis=0))

    @pl.when(j == 0)
    def _():
        kn, vn = kn_ref[0], vn_ref[0]
        qq = lax.broadcasted_iota(jnp.int32, (rows, 1), 0) // nchunk
        ss = []
        for jn in range(nq):
            s = jnp.sum(qbd * kn[jn:jn + 1], axis=1, keepdims=True)
            if mode == "fox":
                s = s + (rowb - pick(jnp.broadcast_to(cn[jn], (rows, nchunk))))
            ss.append(jnp.where(jn <= qq, s, -jnp.inf))
        m = ss[0]
        for s in ss[1:]:
            m = jnp.maximum(m, s)
        ps = [jnp.exp(s - m) for s in ss]
        l = ps[0]
        acc = ps[0] * vn[0:1]
        for jn in range(1, nq):
            l = l + ps[jn]
            acc = acc + ps[jn] * vn[jn:jn + 1]
        m_s[...] = m
        l_s[...] = l
        acc_s[...] = acc
        if mode == "fox":
            carry_s[...] = jnp.zeros(carry_s.shape, F32)

    if mode == "fox":
        kj = lax.broadcasted_iota(jnp.int32, (page, page), 0)
        ks = lax.broadcasted_iota(jnp.int32, (page, page), 1)
        later = (kj > ks).astype(F32)

    for g in range(n_grp):
        s = lax.dot_general(qbd_b, kp[g][0].astype(BF16), NT_DIMS, preferred_element_type=F32)
        if mode == "fox":
            xt = lfp[g][0].T
            tail = jnp.dot(xt, later, preferred_element_type=F32, precision=lax.Precision.HIGHEST) + carry_s[...]
            carry_s[...] = carry_s[...] + jnp.sum(xt, axis=1, keepdims=True)
            s = s + rowb + jnp.concatenate([tail] * nq, axis=0)
        m_old = m_s[...]
        m_new = jnp.maximum(m_old, jnp.max(s, axis=1, keepdims=True))
        alpha = jnp.exp(m_old - m_new)
        p = jnp.exp(s - m_new)
        l_s[...] = alpha * l_s[...] + jnp.sum(p, axis=1, keepdims=True)
        acc_s[...] = alpha * acc_s[...] + _dot(p.astype(BF16), vp[g][0].astype(BF16))
        m_s[...] = m_new

    @pl.when(j == pl.num_programs(1) - 1)
    def _():
        o = acc_s[...] / l_s[...]
        c8 = lax.broadcasted_iota(jnp.int32, (nchunk, w), 0)
        l8w = lax.broadcasted_iota(jnp.int32, (nchunk, w), 1)
        if mode == "diff":
            sign = jnp.where(c8 % 2 == 0, 1.0, -_lam(lq1, lk1, lq2, lk2))
            coef = jnp.where(l8w // (2 * HEAD_DIM) == c8 // 2, sign, 0.0)
        else:
            coef = jnp.where(l8w // HEAD_DIM == c8, 1.0, 0.0)
        o4 = jnp.sum(o.reshape(nq, nchunk, w) * coef[None], axis=1)
        if mode == "diff":
            hw = 2 * HEAD_DIM
            o4 = jnp.concatenate(
                [_rms(o4[:, h * hw:(h + 1) * hw], SUBLN_EPS) * gs_ref[...] for h in range(w // hw)],
                axis=1) * (1.0 - LAMBDA_INIT)
        o_ref[0] = o4.astype(BF16)


def _decode_page_map(b, j, pt, *, g, n_grp, n_pages):
    return (pt[b, n_pages - 1 - (j * n_grp + g)], 0, 0)


def _decode_attention(page_table, q, k_new, v_new, extra, cache_k, cache_v, cache_lf, mode):
    db, nq, w = q.shape
    n_pages = page_table.shape[1]
    n_pool, page = cache_k.shape[0], cache_k.shape[1]
    n_grp = min(DECODE_PAGES, n_pages)
    seq = lambda a: pl.BlockSpec((1,) + a.shape[1:], lambda b, j, pt: (b,) + (0,) * (a.ndim - 1))
    small = lambda a: pl.BlockSpec(a.shape, lambda b, j, pt: (0,) * a.ndim)
    pages = lambda a: [pl.BlockSpec((1,) + a.shape[1:],
                                    functools.partial(_decode_page_map, g=g, n_grp=n_grp, n_pages=n_pages))
                       for g in range(n_grp)]
    in_specs = [seq(q), seq(k_new), seq(v_new)]
    if mode == "diff":
        in_specs += [small(a) for a in extra]
    else:
        in_specs += [seq(extra[0])]
    in_specs += pages(cache_k) + pages(cache_v)
    args = [q, k_new, v_new, *extra] + [cache_k] * n_grp + [cache_v] * n_grp
    nchunk = w // HEAD_DIM
    rows = nq * nchunk
    scratch = [pltpu.VMEM((rows, 1), F32), pltpu.VMEM((rows, 1), F32), pltpu.VMEM((rows, w), F32)]
    if mode == "fox":
        in_specs += pages(cache_lf)
        args += [cache_lf] * n_grp
        scratch.append(pltpu.VMEM((nchunk, 1), F32))
    return pl.pallas_call(
        functools.partial(_decode_kernel, n_grp=n_grp, mode=mode),
        grid_spec=pltpu.PrefetchScalarGridSpec(
            num_scalar_prefetch=1,
            grid=(db, n_pages // n_grp),
            in_specs=in_specs,
            out_specs=pl.BlockSpec((1, nq, w), lambda b, j, pt: (b, 0, 0)),
            scratch_shapes=scratch),
        out_shape=jax.ShapeDtypeStruct((db, nq, w), BF16),
        compiler_params=pltpu.CompilerParams(dimension_semantics=("arbitrary", "arbitrary"),
                                             vmem_limit_bytes=VMEM_LIMIT),
        name="sample_attention_" + mode,
    )(page_table, *args)


def _merge_kernel(oa_ref, ob_ref, ga_ref, gb_ref, x_ref, g1_ref, sh2_ref, sc2_ref,
                  wa_ref, wb_ref, wo_ref, gpost_ref, gpre_ref, wr_hi_ref, wr_lo_ref, br_ref, cnt_in_ref,
                  *rest, n_alias):
    x1_ref, h2_ref, idx_ref, w_ref, rank_ref, cnt_ref, carry_s = rest[n_alias:]
    i = pl.program_id(0)

    @pl.when(i == 0)
    def _():
        carry_s[...] = cnt_in_ref[...]

    d = x_ref.shape[-1]
    y = ga_ref[...] * _dot(oa_ref[...], wa_ref[...]) + gb_ref[...] * _dot(ob_ref[...], wb_ref[...])
    y = _dot(y.astype(BF16), wo_ref[...])
    x1 = x_ref[...] + g1_ref[...].reshape(-1, d) * (_rms(y, NORM_EPS) * gpost_ref[...])
    x1_ref[...] = x1
    h = _rms(x1, NORM_EPS) * gpre_ref[...]
    h = h * (1.0 + sc2_ref[...].reshape(-1, d)) + sh2_ref[...].reshape(-1, d)
    h2_ref[...] = h

    h_hi, h_lo = _split_bf16(h)
    nt = lambda a, b: lax.dot_general(a, b, NT_DIMS, preferred_element_type=F32)
    logit = nt(wr_hi_ref[...], h_hi) + nt(wr_hi_ref[...], h_lo) + nt(wr_lo_ref[...], h_hi) + br_ref[...]
    ne, tm = logit.shape
    e_iota = lax.broadcasted_iota(jnp.int32, (ne, tm), 0)
    idxs, vals = [], []
    for _ in range(TOP_K):
        mx = jnp.max(logit, axis=0, keepdims=True)
        ik = jnp.min(jnp.where(logit == mx, e_iota, ne), axis=0, keepdims=True)
        idxs.append(ik)
        vals.append(mx)
        logit = jnp.where(e_iota == ik, -jnp.inf, logit)
    ex = [jnp.exp(v - vals[0]) for v in vals]
    den = ex[0]
    for e in ex[1:]:
        den = den + e
    onehot = jnp.zeros((ne, tm), F32)
    for ik in idxs:
        onehot = onehot + (e_iota == ik).astype(F32)
    tr = lax.broadcasted_iota(jnp.int32, (tm, tm), 0)
    tc = lax.broadcasted_iota(jnp.int32, (tm, tm), 1)
    before = _dot(onehot.astype(BF16), (tr < tc).astype(BF16)) + carry_s[...]
    for k in range(TOP_K):
        idx_ref[k:k + 1, :] = idxs[k]
        w_ref[k:k + 1, :] = ex[k] / den
        rank_ref[k:k + 1, :] = jnp.sum(jnp.where(e_iota == idxs[k], before, 0.0), axis=0, keepdims=True).astype(jnp.int32)
    carry_s[...] = carry_s[...] + jnp.sum(onehot, axis=1, keepdims=True)
    cnt_ref[...] = carry_s[...]


def _merge_route(oa, ob, gates, x2d, g1, sh2, sc2, wa, wb, wo, gpost, gpre, wr_hi, wr_lo, br, cnt_in,
                 n_total, row_offset, prev, rows_per_seq):
    n, d = x2d.shape
    tm = min(TOKEN_TILE, n)
    off = row_offset // tm
    if g1.ndim == 3:
        tps = rows_per_seq // tm
        mod_spec = pl.BlockSpec((1, 1, d), lambda i: (i // tps, 0, 0))
    else:
        mod_spec = pl.BlockSpec((tm, d), lambda i: (i, 0))
    const = lambda a: pl.BlockSpec(a.shape, lambda i: (0, 0), pipeline_mode=pl.Buffered(1))
    row = lambda w, c=0: pl.BlockSpec((tm, w), lambda i: (i, c))
    anyspec = pl.BlockSpec(memory_space=pl.ANY)
    n_alias = len(prev)
    out_shape = [jax.ShapeDtypeStruct((n_total, d), F32), jax.ShapeDtypeStruct((n_total, d), F32),
                 jax.ShapeDtypeStruct((TOP_K, n_total), jnp.int32), jax.ShapeDtypeStruct((TOP_K, n_total), F32),
                 jax.ShapeDtypeStruct((TOP_K, n_total), jnp.int32), jax.ShapeDtypeStruct((N_EXPERTS, 1), F32)]
    out_specs = [pl.BlockSpec((tm, d), lambda i: (i + off, 0)), pl.BlockSpec((tm, d), lambda i: (i + off, 0)),
                 pl.BlockSpec((TOP_K, tm), lambda i: (0, i + off)), pl.BlockSpec((TOP_K, tm), lambda i: (0, i + off)),
                 pl.BlockSpec((TOP_K, tm), lambda i: (0, i + off)), pl.BlockSpec((N_EXPERTS, 1), lambda i: (0, 0))]
    n_in = 17
    return pl.pallas_call(
        functools.partial(_merge_kernel, n_alias=n_alias),
        grid=(n // tm,),
        in_specs=[row(oa.shape[1]), row(ob.shape[1]), row(d, 0), row(d, 1), row(d), mod_spec, mod_spec, mod_spec,
                  const(wa), const(wb), const(wo), const(gpost), const(gpre), const(wr_hi), const(wr_lo), const(br),
                  const(cnt_in)] + [anyspec] * n_alias,
        out_specs=out_specs,
        out_shape=out_shape,
        input_output_aliases={n_in + a: a for a in range(n_alias)},
        scratch_shapes=[pltpu.VMEM((N_EXPERTS, 1), F32)],
        compiler_params=pltpu.CompilerParams(dimension_semantics=("arbitrary",), vmem_limit_bytes=VMEM_LIMIT),
        name="merge_and_route",
    )(oa, ob, gates, gates, x2d, g1, sh2, sc2, wa, wb, wo, gpost, gpre, wr_hi, wr_lo, br, cnt_in, *prev)


def _dest_kernel(idx_ref, rank_ref, start_ref, o_ref):
    idx = idx_ref[...]
    e_iota = lax.broadcasted_iota(jnp.int32, (N_EXPERTS, idx.shape[1]), 0)
    for k in range(TOP_K):
        base = jnp.sum(jnp.where(e_iota == idx[k:k + 1], start_ref[...], 0), axis=0, keepdims=True)
        o_ref[k:k + 1, :] = base + rank_ref[k:k + 1, :]


def _dest_slots(idx, rank, pad_start):
    n = idx.shape[1]
    tn = min(2048, n)
    while n % tn:
        tn //= 2
    spec = pl.BlockSpec((TOP_K, tn), lambda i: (0, i))
    return pl.pallas_call(
        _dest_kernel, grid=(n // tn,),
        in_specs=[spec, spec, pl.BlockSpec((N_EXPERTS, 1), lambda i: (0, 0))],
        out_specs=spec, out_shape=jax.ShapeDtypeStruct((TOP_K, n), jnp.int32),
        name="moe_slots",
    )(idx, rank, pad_start)


def _dispatch_kernel(dest_ref, h_ref, xd_ref, sem):
    tm = h_ref.shape[0]

    def body(t, carry):
        for k in range(TOP_K):
            pltpu.make_async_copy(h_ref.at[pl.ds(t, 1), :], xd_ref.at[pl.ds(dest_ref[k, t], 1), :], sem).start()
        return carry

    lax.fori_loop(0, tm, body, 0)
    for _ in range(TOP_K):
        pltpu.make_async_copy(h_ref, xd_ref.at[pl.ds(0, tm), :], sem).wait()


def _dispatch(dest, h2, n_slots):
    n, d = h2.shape
    tm = min(TOKEN_TILE, n)
    while n % tm:
        tm //= 2
    return pl.pallas_call(
        _dispatch_kernel, grid=(n // tm,),
        in_specs=[pl.BlockSpec((TOP_K, tm), lambda i: (0, i), memory_space=pltpu.SMEM),
                  pl.BlockSpec((tm, d), lambda i: (i, 0))],
        out_specs=pl.BlockSpec(memory_space=pl.ANY),
        out_shape=jax.ShapeDtypeStruct((n_slots, d), F32),
        scratch_shapes=[pltpu.SemaphoreType.DMA(())],
        compiler_params=pltpu.CompilerParams(dimension_semantics=("arbitrary",)),
        name="moe_dispatch",
    )(dest, h2)


def _expert_kernel(be_ref, nv_ref, nu_ref, x_ref, w1_ref, b1_ref, w2_ref, b2_ref, y_ref):
    del be_ref
    i = pl.program_id(0)

    @pl.when(i < nu_ref[0])
    def _():
        x = x_ref[...]
        row = lax.broadcasted_iota(jnp.int32, (x.shape[0], 1), 0)
        x = jnp.where(row < nv_ref[i], x, 0.0).astype(BF16)
        hh = _dot(x, w1_ref[0]) + b1_ref[0]
        f = hh.shape[1] // 2
        x_glu = jnp.minimum(hh[:, :f], SWIGLU_LIMIT)
        x_lin = jnp.clip(hh[:, f:], -SWIGLU_LIMIT, SWIGLU_LIMIT)
        act = x_glu * jax.nn.sigmoid(SWIGLU_ALPHA * x_glu) * (x_lin + 1.0)
        y_ref[...] = _dot(act.astype(BF16), w2_ref[0]) + b2_ref[0]


def _experts(block_expert, n_valid, n_used, x_disp, w1, b1, w2, b2):
    n_slots, d = x_disp.shape
    tb = EXPERT_TILE
    ne, _, f2 = w1.shape
    blk = lambda i, be, nv, nu: (jnp.minimum(i, nu[0] - 1), 0)
    wsel = lambda i, be, nv, nu: (be[i], 0, 0)
    return pl.pallas_call(
        _expert_kernel,
        grid_spec=pltpu.PrefetchScalarGridSpec(
            num_scalar_prefetch=3,
            grid=(n_slots // tb,),
            in_specs=[pl.BlockSpec((tb, d), blk),
                      pl.BlockSpec((1, d, f2), wsel), pl.BlockSpec((1, 1, f2), wsel),
                      pl.BlockSpec((1, f2 // 2, d), wsel), pl.BlockSpec((1, 1, d), wsel)],
            out_specs=pl.BlockSpec((tb, d), blk)),
        out_shape=jax.ShapeDtypeStruct((n_slots, d), F32),
        compiler_params=pltpu.CompilerParams(dimension_semantics=("arbitrary",), vmem_limit_bytes=VMEM_LIMIT),
        name="moe_experts",
    )(block_expert, n_valid, n_used, x_disp, w1, b1.reshape(ne, 1, f2), w2, b2.reshape(ne, 1, d))


def _combine_kernel(dest_ref, w_ref, x1_ref, g2_ref, gpost_ref, yd_ref, o_ref, buf, sem):
    tm, d = x1_ref.shape

    def body(t, carry):
        for k in range(TOP_K):
            pltpu.make_async_copy(yd_ref.at[pl.ds(dest_ref[k, t], 1), :], buf.at[k, pl.ds(t, 1), :], sem).start()
        return carry

    lax.fori_loop(0, tm, body, 0)
    for k in range(TOP_K):
        pltpu.make_async_copy(yd_ref.at[pl.ds(0, tm), :], buf.at[k], sem).wait()
    wt = w_ref[...].T
    y = wt[:, 0:1] * buf[0]
    for k in range(1, TOP_K):
        y = y + wt[:, k:k + 1] * buf[k]
    o_ref[...] = x1_ref[...] + g2_ref[...].reshape(-1, d) * (_rms(y, NORM_EPS) * gpost_ref[...])


def _combine(dest, w, x1, g2, gpost, y_disp, n, row_offset, rows_per_seq):
    d = x1.shape[1]
    tm = min(TOKEN_TILE, n)
    off = row_offset // tm
    if g2.ndim == 3:
        tps = rows_per_seq // tm
        mod_spec = pl.BlockSpec((1, 1, d), lambda i: (i // tps, 0, 0))
    else:
        mod_spec = pl.BlockSpec((tm, d), lambda i: (i, 0))
    return pl.pallas_call(
        _combine_kernel, grid=(n // tm,),
        in_specs=[pl.BlockSpec((TOP_K, tm), lambda i: (0, i + off), memory_space=pltpu.SMEM),
                  pl.BlockSpec((TOP_K, tm), lambda i: (0, i + off)),
                  pl.BlockSpec((tm, d), lambda i: (i + off, 0)),
                  mod_spec,
                  pl.BlockSpec((1, d), lambda i: (0, 0)),
                  pl.BlockSpec(memory_space=pl.ANY)],
        out_specs=pl.BlockSpec((tm, d), lambda i: (i, 0)),
        out_shape=jax.ShapeDtypeStruct((n, d), F32),
        scratch_shapes=[pltpu.VMEM((TOP_K, tm, d), F32), pltpu.SemaphoreType.DMA(())],
        compiler_params=pltpu.CompilerParams(dimension_semantics=("arbitrary",), vmem_limit_bytes=VMEM_LIMIT),
        name="moe_combine",
    )(dest, w, x1, g2, gpost, y_disp)


def kernel(x_prompt, x_sample, c_prompt, c_sample, cache_a_k, cache_a_v, cache_b_k, cache_b_v, cache_b_logf, page_table, w_mod, b_mod, g_pre_mix, g_post_mix, g_pre_ffn, g_post_ffn, w_in, b_forget, b_gate, lambda_q1, lambda_k1, lambda_q2, lambda_k2, g_subln, w_br_a, w_br_b, w_out, w_router, b_router, w_mlp1, b_mlp1, w_mlp2, b_mlp2):
    bsz, seq, d = x_prompt.shape
    db, nq, _ = x_sample.shape
    n_pool, page = cache_a_k.shape[1], cache_a_k.shape[2]
    n_pages = page_table.shape[1]
    past_len = n_pages * page
    a_heads = cache_a_k.shape[3]
    b_heads = cache_b_k.shape[3]
    n_p, n_s = bsz * seq, db * nq
    n_tot = n_p + n_s
    w = QKV_WIDTH

    w_in0 = w_in[0]
    wqkv = w_in0[:, :6 * w].astype(BF16)
    wf = jnp.pad(w_in0[:, 6 * w:6 * w + b_heads], ((0, 0), (0, LANES - b_heads))).astype(BF16)
    wg = w_in0[:, 6 * w + b_heads:].astype(BF16)
    bf = jnp.pad(b_forget[0], (0, LANES - b_heads)).reshape(1, LANES)
    bg = b_gate[0].reshape(1, -1)
    row = lambda a: a[0].reshape(1, -1)
    lam_args = (row(lambda_q1), row(lambda_k1), row(lambda_q2), row(lambda_k2), row(g_subln))
    wr_t = w_router[0].T
    wr_hi = wr_t.astype(BF16)
    wr_lo = (wr_t - wr_hi.astype(F32)).astype(BF16)
    br = b_router[0].reshape(N_EXPERTS, 1)

    mod = _adaln(jnp.concatenate([c_prompt, c_sample], axis=0), w_mod[0], b_mod[0]).reshape(bsz + db, 6, d)
    mod_p = [mod[:bsz, i].reshape(bsz, 1, d) for i in range(6)]
    mod_s = [jnp.repeat(mod[bsz:, i], nq, axis=0) for i in range(6)]

    tabs_p = _rope_tables(jnp.arange(seq, dtype=F32))
    tabs_s = tuple(jnp.tile(t, (db, 1)) for t in _rope_tables(past_len + jnp.arange(nq, dtype=F32)))
    xp2 = x_prompt.reshape(n_p, d)
    xs2 = x_sample.reshape(n_s, d)
    qa, ka, va, qb, kb, vb, lf, gates = _project(
        xp2, mod_p[0], mod_p[1], row(g_pre_mix), tabs_p, wqkv, wf, wg, bf, bg, b_heads, seq)
    qa_s, ka_s, va_s, qb_s, kb_s, vb_s, lf_s, gates_s = _project(
        xs2, mod_s[0], mod_s[1], row(g_pre_mix), tabs_s, wqkv, wf, wg, bf, bg, b_heads, nq)

    sh3 = lambda a: a.reshape(bsz, seq, w)
    oa = _prompt_attention(sh3(qa), sh3(ka), sh3(va), lam_args, "diff")
    cum, cum_t = _cum_logf(lf.reshape(bsz, seq, b_heads))
    ob = _prompt_attention(sh3(qb), sh3(kb), sh3(vb), (cum, cum_t), "fox")

    sd3 = lambda a: a.reshape(db, nq, -1)
    pool = lambda c: c[0].reshape(n_pool, page, -1)
    oa_s = _decode_attention(page_table, sd3(qa_s), sd3(ka_s), sd3(va_s), lam_args,
                             pool(cache_a_k), pool(cache_a_v), None, "diff")
    ob_s = _decode_attention(page_table, sd3(qb_s), sd3(kb_s), sd3(vb_s), (sd3(lf_s),),
                             pool(cache_b_k), pool(cache_b_v), pool(cache_b_logf), "fox")

    wa, wb, wo = w_br_a[0].astype(BF16), w_br_b[0].astype(BF16), w_out[0].astype(BF16)
    common = (wa, wb, wo, row(g_post_mix), row(g_pre_ffn), wr_hi, wr_lo, br)
    outs_p = _merge_route(oa.reshape(n_p, w), ob.reshape(n_p, w), gates, xp2, mod_p[2], mod_p[3], mod_p[4], *common,
                          jnp.zeros((N_EXPERTS, 1), F32), n_tot, 0, (), seq)
    x1, h2, idx, wts, rank, counts = _merge_route(
        oa_s.reshape(n_s, w), ob_s.reshape(n_s, w), gates_s, xs2, mod_s[2], mod_s[3], mod_s[4], *common,
        outs_p[5], n_tot, n_p, tuple(outs_p[:5]), nq)

    tb = EXPERT_TILE
    n_blocks = (n_tot * TOP_K + N_EXPERTS * (tb - 1)) // tb
    cnt = counts[:, 0].astype(jnp.int32)
    padded = (cnt + tb - 1) // tb * tb
    pad_end = jnp.cumsum(padded)
    pad_start = pad_end - padded
    blk_start = jnp.arange(n_blocks, dtype=jnp.int32) * tb
    block_expert = jnp.minimum(jnp.searchsorted(pad_end, blk_start, side="right"), N_EXPERTS - 1).astype(jnp.int32)
    n_valid = jnp.clip(cnt[block_expert] - (blk_start - pad_start[block_expert]), 0, tb).astype(jnp.int32)
    n_used = (pad_end[-1:] // tb).astype(jnp.int32)

    dest = _dest_slots(idx, rank, pad_start.reshape(N_EXPERTS, 1).astype(jnp.int32))
    x_disp = _dispatch(dest, h2, n_blocks * tb)
    y_disp = _experts(block_expert, n_valid, n_used, x_disp,
                      w_mlp1[0].astype(BF16), b_mlp1[0], w_mlp2[0].astype(BF16), b_mlp2[0])
    y_p = _combine(dest, wts, x1, mod_p[5], row(g_post_ffn), y_disp, n_p, 0, seq)
    y_s = _combine(dest, wts, x1, mod_s[5], row(g_post_ffn), y_disp, n_s, n_p, nq)

    return (y_p.reshape(bsz, seq, d), y_s.reshape(db, nq, d),
            ka.reshape(1, bsz, seq, a_heads, 2, HEAD_DIM), va.reshape(1, bsz, seq, a_heads, 2 * HEAD_DIM),
            kb.reshape(1, bsz, seq, b_heads, HEAD_DIM), vb.reshape(1, bsz, seq, b_heads, HEAD_DIM),
            lf.reshape(1, bsz, seq, b_heads),
            ka_s.reshape(1, db, nq, a_heads, 2, HEAD_DIM), va_s.reshape(1, db, nq, a_heads, 2 * HEAD_DIM),
            kb_s.reshape(1, db, nq, b_heads, HEAD_DIM), vb_s.reshape(1, db, nq, b_heads, HEAD_DIM),
            lf_s.reshape(1, db, nq, b_heads))
```

```python
import functools

import jax
import jax.numpy as jnp
from jax import lax
from jax.experimental import pallas as pl
from jax.experimental.pallas import tpu as pltpu

F32 = jnp.float32
BF16 = jnp.bfloat16

HEAD_DIM = 64
ROT_DIM = HEAD_DIM // 4
ROPE_THETA = 500000.0
ATTN_SCALE = HEAD_DIM ** -0.5
N_EXPERTS = 32
TOP_K = 4
SWIGLU_LIMIT = 7.0
SWIGLU_ALPHA = 1.702
NORM_EPS = 1e-6
SUBLN_EPS = 1e-5
LAMBDA_INIT = 0.8 - 0.6

LANES = 128
QKV_WIDTH = 512
TOKEN_TILE = 256
ATTN_TILE = 256
EXPERT_TILE = 512
DECODE_PAGES = 16
VMEM_LIMIT = 56 * 1024 * 1024

NT_DIMS = (((1,), (1,)), ((), ()))


def _dot(a, b):
    return jnp.dot(a, b, preferred_element_type=F32)


def _split_bf16(x):
    hi = x.astype(BF16)
    lo = (x - hi.astype(F32)).astype(BF16)
    return hi, lo


def _rms(x, eps):
    return x * lax.rsqrt(jnp.mean(x * x, axis=-1, keepdims=True) + eps)


def _lam(lq1, lk1, lq2, lk2):
    a = jnp.exp(jnp.sum(lq1[...] * lk1[...], axis=1, keepdims=True))
    b = jnp.exp(jnp.sum(lq2[...] * lk2[...], axis=1, keepdims=True))
    return a - b + LAMBDA_INIT


def _mod_kernel(c_ref, w_ref, b_ref, o_ref):
    c = c_ref[...]
    s_hi, s_lo = _split_bf16(c * jax.nn.sigmoid(c))
    w_hi, w_lo = _split_bf16(w_ref[...])
    o_ref[...] = _dot(s_hi, w_hi) + _dot(s_lo, w_hi) + _dot(s_hi, w_lo) + b_ref[...]


def _adaln(c, w_mod, b_mod):
    n, d = c.shape
    width = w_mod.shape[1]
    return pl.pallas_call(
        _mod_kernel,
        grid=(width // d,),
        in_specs=[pl.BlockSpec((n, d), lambda j: (0, 0)),
                  pl.BlockSpec((d, d), lambda j: (0, j)),
                  pl.BlockSpec((1, d), lambda j: (0, j))],
        out_specs=pl.BlockSpec((n, d), lambda j: (0, j)),
        out_shape=jax.ShapeDtypeStruct((n, width), F32),
        name="adaln_mod",
    )(c, w_mod, b_mod.reshape(1, width))


def _normed_input(x_ref, shift_ref, scale_ref, g_ref):
    x = x_ref[...]
    d = x.shape[-1]
    h = _rms(x, NORM_EPS) * g_ref[...]
    return (h * (1.0 + scale_ref[...].reshape(-1, d)) + shift_ref[...].reshape(-1, d)).astype(BF16)


def _rope_lanes(y, cos, sa, sb):
    parts = []
    for g in range(y.shape[1] // LANES):
        yg = y[:, g * LANES:(g + 1) * LANES]
        parts.append(yg * cos + pltpu.roll(yg, ROT_DIM // 2, 1) * sa
                     + pltpu.roll(yg, LANES - ROT_DIM // 2, 1) * sb)
    return jnp.concatenate(parts, axis=1)


def _log_sigmoid(z):
    return jnp.minimum(z, 0.0) - jnp.log1p(jnp.exp(-jnp.abs(z)))


def _proj_kernel(x_ref, shift_ref, scale_ref, g_ref, cos_ref, sa_ref, sb_ref,
                 wqkv_ref, wf_ref, wg_ref, bf_ref, bg_ref,
                 qa_ref, ka_ref, va_ref, qb_ref, kb_ref, vb_ref, lf_ref, gate_ref):
    hb = _normed_input(x_ref, shift_ref, scale_ref, g_ref)
    cos, sa, sb = cos_ref[...], sa_ref[...], sb_ref[...]

    def seg(i):
        return _dot(hb, wqkv_ref[:, i * QKV_WIDTH:(i + 1) * QKV_WIDTH])

    qa_ref[...] = (_rope_lanes(seg(0), cos, sa, sb) * ATTN_SCALE).astype(BF16)
    ka_ref[...] = _rope_lanes(seg(1), cos, sa, sb)
    va_ref[...] = seg(2)
    qb_ref[...] = (seg(3) * ATTN_SCALE).astype(BF16)
    kb_ref[...] = seg(4)
    vb_ref[...] = seg(5)
    z = (_dot(hb, wf_ref[...]) + bf_ref[...])[:, :lf_ref.shape[-1]]
    lf_ref[...] = _log_sigmoid(z)
    gate_ref[...] = jax.nn.sigmoid(_dot(hb, wg_ref[...]) + bg_ref[...]).astype(BF16)


def _project_sample(x2d, shift, scale, g_pre, tabs, wqkv, wf, wg, bf, bg, n_heads_b):
    n, d = x2d.shape
    tm = min(TOKEN_TILE, n)
    const = lambda a: pl.BlockSpec(a.shape, lambda i: (0, 0), pipeline_mode=pl.Buffered(1))
    row = lambda w: pl.BlockSpec((tm, w), lambda i: (i, 0))
    gw = wg.shape[1]
    return pl.pallas_call(
        _proj_kernel,
        grid=(n // tm,),
        in_specs=[row(d), row(d), row(d), const(g_pre), row(LANES), row(LANES), row(LANES),
                  const(wqkv), const(wf), const(wg), const(bf), const(bg)],
        out_specs=[row(QKV_WIDTH)] * 6 + [row(n_heads_b), row(gw)],
        out_shape=[jax.ShapeDtypeStruct((n, QKV_WIDTH), BF16),
                   jax.ShapeDtypeStruct((n, QKV_WIDTH), F32),
                   jax.ShapeDtypeStruct((n, QKV_WIDTH), F32),
                   jax.ShapeDtypeStruct((n, QKV_WIDTH), BF16),
                   jax.ShapeDtypeStruct((n, QKV_WIDTH), F32),
                   jax.ShapeDtypeStruct((n, QKV_WIDTH), F32),
                   jax.ShapeDtypeStruct((n, n_heads_b), F32),
                   jax.ShapeDtypeStruct((n, gw), BF16)],
        compiler_params=pltpu.CompilerParams(dimension_semantics=("arbitrary",), vmem_limit_bytes=VMEM_LIMIT),
        name="sample_projection",
    )(x2d, shift, scale, g_pre, *tabs, wqkv, wf, wg, bf, bg)


def _proj_prompt_kernel(x_ref, shift_ref, scale_ref, g_ref, cos_ref, sa_ref, sb_ref, cost_ref, sint_ref,
                        wq_ref, wv_ref, wt_ref, wft_ref, wg_ref, bft_ref, bg_ref,
                        qa_ref, kat_ref, va_ref, vab_ref, qb_ref, kbt_ref, vbt_ref, lft_ref, gate_ref):
    hb = _normed_input(x_ref, shift_ref, scale_ref, g_ref)
    hbt = hb.T
    w = QKV_WIDTH
    qa = _rope_lanes(_dot(hb, wq_ref[:, :w]), cos_ref[...], sa_ref[...], sb_ref[...])
    qa_ref[...] = (qa * ATTN_SCALE).astype(BF16)
    qb_ref[...] = (_dot(hb, wq_ref[:, w:]) * ATTN_SCALE).astype(BF16)
    v = _dot(hb, wv_ref[...])
    vab_ref[...] = v.astype(BF16)
    for h in range(va_ref.shape[1]):
        va_ref[:, h, :] = v[:, h * LANES:(h + 1) * LANES]

    kat = _dot(wt_ref[0:w, :], hbt)
    cos_t, sin_t = cost_ref[...], sint_ref[...]
    half = ROT_DIM // 2
    parts = []
    for c in range(w // HEAD_DIM):
        base = c * HEAD_DIM
        x1 = kat[base:base + half]
        x2 = kat[base + half:base + ROT_DIM]
        parts += [x1 * cos_t - x2 * sin_t, x2 * cos_t + x1 * sin_t, kat[base + ROT_DIM:base + HEAD_DIM]]
    kat_ref[0] = jnp.concatenate(parts, axis=0)
    kbt_ref[0] = _dot(wt_ref[w:2 * w, :], hbt)
    vbt_ref[0] = _dot(wt_ref[2 * w:3 * w, :], hbt)
    nh = lft_ref.shape[1]
    lft_ref[0] = _log_sigmoid(_dot(wft_ref[...], hbt)[:nh] + bft_ref[...])
    gate_ref[...] = jax.nn.sigmoid(_dot(hb, wg_ref[...]) + bg_ref[...]).astype(BF16)


def _project_prompt(x2d, shift, scale, g_pre, tabs, tabs_t, wq, wv, wt, wft, wg, bft, bg, bsz, seq, a_heads, b_heads):
    n, d = x2d.shape
    tm = min(TOKEN_TILE, seq)
    tps = seq // tm
    w = QKV_WIDTH
    const = lambda a: pl.BlockSpec(a.shape, lambda i: (0, 0), pipeline_mode=pl.Buffered(1))
    row = lambda width: pl.BlockSpec((tm, width), lambda i: (i, 0))
    mod_spec = pl.BlockSpec((1, 1, d), lambda i: (i // tps, 0, 0))
    tab_spec = pl.BlockSpec((tm, LANES), lambda i: (i % tps, 0))
    tabt_spec = pl.BlockSpec((ROT_DIM // 2, tm), lambda i: (0, i % tps))
    tr_spec = lambda rows: pl.BlockSpec((1, rows, tm), lambda i: (i // tps, 0, i % tps))
    gw = wg.shape[1]
    return pl.pallas_call(
        _proj_prompt_kernel,
        grid=(n // tm,),
        in_specs=[row(d), mod_spec, mod_spec, const(g_pre), tab_spec, tab_spec, tab_spec, tabt_spec, tabt_spec,
                  const(wq), const(wv), const(wt), const(wft), const(wg), const(bft), const(bg)],
        out_specs=[row(w), tr_spec(w), pl.BlockSpec((tm, a_heads, LANES), lambda i: (i, 0, 0)), row(w),
                   row(w), tr_spec(w), tr_spec(w), tr_spec(b_heads), row(gw)],
        out_shape=[jax.ShapeDtypeStruct((n, w), BF16),
                   jax.ShapeDtypeStruct((bsz, w, seq), F32),
                   jax.ShapeDtypeStruct((n, a_heads, LANES), F32),
                   jax.ShapeDtypeStruct((n, w), BF16),
                   jax.ShapeDtypeStruct((n, w), BF16),
                   jax.ShapeDtypeStruct((bsz, w, seq), F32),
                   jax.ShapeDtypeStruct((bsz, w, seq), F32),
                   jax.ShapeDtypeStruct((bsz, b_heads, seq), F32),
                   jax.ShapeDtypeStruct((n, gw), BF16)],
        compiler_params=pltpu.CompilerParams(dimension_semantics=("arbitrary",), vmem_limit_bytes=VMEM_LIMIT),
        name="prompt_projection",
    )(x2d, shift, scale, g_pre, *tabs, *tabs_t, wq, wv, wt, wft, wg, bft, bg)


def _rope_angles(pos):
    half = ROT_DIM // 2
    inv_freq = jnp.power(ROPE_THETA, -jnp.arange(half, dtype=F32) * 2.0 / ROT_DIM)
    return pos[:, None] * inv_freq[None, :]


def _rope_tables(pos):
    half = ROT_DIM // 2
    ang = _rope_angles(pos)
    cos, sin = jnp.cos(ang), jnp.sin(ang)
    s = pos.shape[0]
    ones = jnp.ones((s, HEAD_DIM - ROT_DIM), F32)
    zeros = jnp.zeros((s, HEAD_DIM - ROT_DIM), F32)
    zh = jnp.zeros((s, half), F32)
    cos_t = jnp.concatenate([cos, cos, ones], axis=1)
    sa_t = jnp.concatenate([zh, sin, zeros], axis=1)
    sb_t = jnp.concatenate([-sin, zh, zeros], axis=1)
    rep = LANES // HEAD_DIM
    return tuple(jnp.tile(t, (1, rep)) for t in (cos_t, sa_t, sb_t))


def _cum_kernel(lft_ref, c_ref, ct_ref, *, chunk):
    nh, s = lft_ref.shape[1], lft_ref.shape[2]
    r = lax.broadcasted_iota(jnp.int32, (chunk, chunk), 0)
    c = lax.broadcasted_iota(jnp.int32, (chunk, chunk), 1)
    upto = (r <= c).astype(F32)
    carry = jnp.zeros((nh, 1), F32)
    for i in range(s // chunk):
        x = lft_ref[0, :, i * chunk:(i + 1) * chunk]
        cum = jnp.dot(x, upto, preferred_element_type=F32, precision=lax.Precision.HIGHEST) + carry
        carry = cum[:, chunk - 1:chunk]
        cum_n = cum.T
        for p in range(nh // 2):
            ct_ref[0, p, :, i * chunk:(i + 1) * chunk] = cum[2 * p:2 * p + 2, :]
            c_ref[0, p, i * chunk:(i + 1) * chunk, :] = cum_n[:, 2 * p:2 * p + 2]


def _cum_logf(lft):
    b, nh, s = lft.shape
    chunk = min(256, s)
    return pl.pallas_call(
        functools.partial(_cum_kernel, chunk=chunk),
        grid=(b,),
        in_specs=[pl.BlockSpec((1, nh, s), lambda i: (i, 0, 0))],
        out_specs=[pl.BlockSpec((1, nh // 2, s, 2), lambda i: (i, 0, 0, 0)),
                   pl.BlockSpec((1, nh // 2, 2, s), lambda i: (i, 0, 0, 0))],
        out_shape=[jax.ShapeDtypeStruct((b, nh // 2, s, 2), F32),
                   jax.ShapeDtypeStruct((b, nh // 2, 2, s), F32)],
        name="forget_cumsum",
    )(lft)


def _pattn_kernel(*refs, tq, mode):
    if mode == "diff":
        q_ref, kt_ref, v_ref, lq1, lk1, lq2, lk2, gs_ref, o_ref, kt_s, v_s, m_s, acc_s = refs
    else:
        q_ref, kt_ref, vt_ref, c_ref, ct_ref, o_ref, kt_s, v_s, m_s, acc_s = refs
    qi = pl.program_id(2)
    tk = 2 * tq

    @pl.when(qi == 0)
    def _():
        kt_s[...] = kt_ref[0].astype(BF16)
        if mode == "diff":
            v_s[:, :LANES] = v_ref[0]
            v_s[:, LANES:] = jnp.ones((v_s.shape[0], LANES), BF16)
        else:
            v = vt_ref[0].T
            first = lax.broadcasted_iota(jnp.int32, v.shape, 1) < HEAD_DIM
            v_s[0] = jnp.where(first, v, 1.0).astype(BF16)
            v_s[1] = jnp.where(first, 1.0, v).astype(BF16)

    q = q_ref[0]
    lane = lax.broadcasted_iota(jnp.int32, q.shape, 1)
    zero = jnp.zeros_like(q)
    qz = (jnp.where(lane < HEAD_DIM, q, zero), jnp.where(lane >= HEAD_DIM, q, zero))
    m_s[...] = jnp.full(m_s.shape, -jnp.inf, F32)
    acc_s[...] = jnp.zeros(acc_s.shape, F32)
    if mode == "fox":
        cb = c_ref[0, 0]
        rowb = tuple(jnp.broadcast_to(cb[:, hf:hf + 1], (tq, LANES)) for hf in range(2))
    row = lax.broadcasted_iota(jnp.int32, (tq, LANES), 0)
    col = lax.broadcasted_iota(jnp.int32, (tq, LANES), 1)
    n_acc = acc_s.shape[2] // LANES

    def tile(off, width, masked):
        kt = kt_s[:, pl.ds(off, width)]
        for hf in range(2):
            vt = v_s[pl.ds(off, width), :] if mode == "diff" else v_s[hf, pl.ds(off, width), :]
            s = _dot(qz[hf], kt)
            chunks = []
            for c in range(width // LANES):
                sc = s[:, c * LANES:(c + 1) * LANES]
                if mode == "fox":
                    sc = sc + (rowb[hf] - ct_ref[0, 0, hf:hf + 1, pl.ds(off + c * LANES, LANES)])
                if masked:
                    sc = jnp.where(col + c * LANES <= row, sc, -jnp.inf)
                chunks.append(sc)
            mx = chunks[0]
            for sc in chunks[1:]:
                mx = jnp.maximum(mx, sc)
            m_old = m_s[hf]
            m_new = jnp.maximum(m_old, jnp.max(mx, axis=1, keepdims=True))
            alpha = jnp.exp(m_old - m_new)
            p = jnp.concatenate([jnp.exp(sc - m_new) for sc in chunks], axis=1).astype(BF16)
            acc_s[hf] = jnp.concatenate([alpha] * n_acc, axis=1) * acc_s[hf] + _dot(p, vt)
            m_s[hf] = m_new

    def body(j, carry):
        tile(pl.multiple_of(j * tk, tk), tk, False)
        return carry

    lax.fori_loop(0, qi // 2, body, 0)

    @pl.when(qi % 2 == 1)
    def _():
        tile(pl.multiple_of((qi - 1) * tq, tq), tq, False)

    tile(pl.multiple_of(qi * tq, tq), tq, True)

    if mode == "diff":
        a0, a1 = acc_s[0], acc_s[1]
        o = a0[:, :LANES] / a0[:, LANES:] - _lam(lq1, lk1, lq2, lk2) * (a1[:, :LANES] / a1[:, LANES:])
        o = _rms(o, SUBLN_EPS) * gs_ref[...] * (1.0 - LAMBDA_INIT)
    else:
        a0, a1 = acc_s[0], acc_s[1]
        o = jnp.where(lane < HEAD_DIM, a0 / pltpu.roll(a0, HEAD_DIM, 1), a1 / pltpu.roll(a1, HEAD_DIM, 1))
    o_ref[0] = o.astype(BF16)


def _prompt_attention(q, kt, v, extra, mode):
    b, s, w = q.shape
    tq = min(ATTN_TILE, s)
    nblk = w // LANES
    qspec = pl.BlockSpec((1, tq, LANES), lambda bi, h, qi: (bi, qi, h))
    ktspec = pl.BlockSpec((1, LANES, s), lambda bi, h, qi: (bi, h, 0))
    if mode == "diff":
        small = lambda a: pl.BlockSpec(a.shape, lambda bi, h, qi: (0, 0))
        in_specs = [qspec, ktspec, pl.BlockSpec((1, s, LANES), lambda bi, h, qi: (bi, 0, h))] + [small(a) for a in extra]
        scratch = [pltpu.VMEM((LANES, s), BF16), pltpu.VMEM((s, 2 * LANES), BF16),
                   pltpu.VMEM((2, tq, LANES), F32), pltpu.VMEM((2, tq, 2 * LANES), F32)]
    else:
        in_specs = [qspec, ktspec, ktspec,
                    pl.BlockSpec((1, 1, tq, 2), lambda bi, h, qi: (bi, h, qi, 0)),
                    pl.BlockSpec((1, 1, 2, s), lambda bi, h, qi: (bi, h, 0, 0))]
        scratch = [pltpu.VMEM((LANES, s), BF16), pltpu.VMEM((2, s, LANES), BF16),
                   pltpu.VMEM((2, tq, LANES), F32), pltpu.VMEM((2, tq, LANES), F32)]
    return pl.pallas_call(
        functools.partial(_pattn_kernel, tq=tq, mode=mode),
        grid=(b, nblk, s // tq),
        in_specs=in_specs,
        out_specs=qspec,
        out_shape=jax.ShapeDtypeStruct((b, s, w), BF16),
        scratch_shapes=scratch,
        compiler_params=pltpu.CompilerParams(dimension_semantics=("arbitrary", "arbitrary", "arbitrary"),
                                             vmem_limit_bytes=VMEM_LIMIT),
        name="prompt_attention_" + mode,
    )(q, kt, v, *extra)


def _decode_kernel(pt_ref, *refs, n_grp, mode):
    del pt_ref
    q_ref, kn_ref, vn_ref = refs[:3]
    pos = 3
    if mode == "diff":
        lq1, lk1, lq2, lk2, gs_ref = refs[pos:pos + 5]
        pos += 5
    else:
        lfn_ref = refs[pos]
        pos += 1
    kp = refs[pos:pos + n_grp]
    vp = refs[pos + n_grp:pos + 2 * n_grp]
    pos += 2 * n_grp
    if mode == "fox":
        lfp = refs[pos:pos + n_grp]
        pos += n_grp
    o_ref, m_s, l_s, acc_s = refs[pos:pos + 4]
    if mode == "fox":
        carry_s = refs[pos + 4]
    j = pl.program_id(1)
    nq, w = q_ref.shape[1], q_ref.shape[2]
    nchunk = w // HEAD_DIM
    rows = nq * nchunk
    page = kp[0].shape[2]
    rep = w // LANES

    def rep_rows(x):
        return jnp.concatenate([jnp.broadcast_to(x[i:i + 1], (nchunk, x.shape[1])) for i in range(nq)], axis=0)

    rid = lax.broadcasted_iota(jnp.int32, (rows, w), 0)
    lid = lax.broadcasted_iota(jnp.int32, (rows, w), 1)
    qbd = jnp.where(lid // HEAD_DIM == rid % nchunk, rep_rows(q_ref[0].astype(F32)), 0.0)
    qbd_b = qbd.astype(BF16)
    if mode == "fox":
        lfn = lfn_ref[0]
        cn = [lfn[0:1]]
        for i in range(1, nq):
            cn.append(cn[-1] + lfn[i:i + 1])
        r8 = lax.broadcasted_iota(jnp.int32, (rows, nchunk), 0)
        l8 = lax.broadcasted_iota(jnp.int32, (rows, nchunk), 1)
        sel = l8 == r8 % nchunk
        pick = lambda x: jnp.sum(jnp.where(sel, x, 0.0), axis=1, keepdims=True)
        rowb = pick(jnp.concatenate([jnp.broadcast_to(c, (nchunk, nchunk)) for c in cn], axis=0))
        rowb_l = jnp.broadcast_to(rowb, (rows, LANES))

    @pl.when(j == 0)
    def _():
        kn, vn = kn_ref[0], vn_ref[0]
        qq = lax.broadcasted_iota(jnp.int32, (rows, 1), 0) // nchunk
        ss = []
        for jn in range(nq):
            s = jnp.sum(qbd * kn[jn:jn + 1], axis=1, keepdims=True)
            if mode == "fox":
                s = s + (rowb - pick(jnp.broadcast_to(cn[jn], (rows, nchunk))))
            ss.append(jnp.where(jn <= qq, s, -jnp.inf))
        m = ss[0]
        for s in ss[1:]:
            m = jnp.maximum(m, s)
        ps = [jnp.exp(s - m) for s in ss]
        l = ps[0]
        acc = ps[0] * vn[0:1]
        for jn in range(1, nq):
            l = l + ps[jn]
            acc = acc + ps[jn] * vn[jn:jn + 1]
        m_s[...] = jnp.broadcast_to(m, m_s.shape)
        l_s[...] = jnp.broadcast_to(l, l_s.shape)
        acc_s[...] = acc
        if mode == "fox":
            carry_s[...] = jnp.zeros(carry_s.shape, F32)

    ss = [_dot(qbd_b, kp[g][0].astype(BF16)) for g in range(n_grp)]
    if mode == "fox":
        kj = lax.broadcasted_iota(jnp.int32, (page, page), 0)
        ks = lax.broadcasted_iota(jnp.int32, (page, page), 1)
        later = (kj > ks).astype(F32)
        carry = carry_s[...]
        for g in range(n_grp):
            x = lfp[g][0]
            tail = jnp.dot(x, later, preferred_element_type=F32, precision=lax.Precision.HIGHEST) + carry
            carry = carry + jnp.sum(x, axis=1, keepdims=True)
            ss[g] = ss[g] + (rowb_l + jnp.concatenate([tail] * nq, axis=0))
        carry_s[...] = jnp.broadcast_to(carry, carry_s.shape)
    mx = ss[0]
    for s in ss[1:]:
        mx = jnp.maximum(mx, s)
    m_old = m_s[...]
    m_new = jnp.maximum(m_old, jnp.max(mx, axis=1, keepdims=True))
    alpha = jnp.exp(m_old - m_new)
    ps = [jnp.exp(s - m_new) for s in ss]
    lsum = ps[0]
    for p in ps[1:]:
        lsum = lsum + p
    l_s[...] = alpha * l_s[...] + jnp.sum(lsum, axis=1, keepdims=True)
    pv = None
    for g in range(n_grp):
        pb = ps[g].astype(BF16)
        if mode == "diff":
            t = jnp.concatenate([_dot(pb, vp[g][0, pl.ds(h, page, stride=rep), :].astype(BF16))
                                 for h in range(rep)], axis=1)
        else:
            t = lax.dot_general(pb, vp[g][0].astype(BF16), NT_DIMS, preferred_element_type=F32)
        pv = t if pv is None else pv + t
    acc_s[...] = jnp.concatenate([alpha] * rep, axis=1) * acc_s[...] + pv
    m_s[...] = m_new

    @pl.when(j == pl.num_programs(1) - 1)
    def _():
        o = acc_s[...] / jnp.concatenate([l_s[...]] * rep, axis=1)
        c8 = lax.broadcasted_iota(jnp.int32, (nchunk, w), 0)
        l8w = lax.broadcasted_iota(jnp.int32, (nchunk, w), 1)
        if mode == "diff":
            sign = jnp.where(c8 % 2 == 0, 1.0, -_lam(lq1, lk1, lq2, lk2))
            coef = jnp.where(l8w // (2 * HEAD_DIM) == c8 // 2, sign, 0.0)
        else:
            coef = jnp.where(l8w // HEAD_DIM == c8, 1.0, 0.0)
        o4 = jnp.sum(o.reshape(nq, nchunk, w) * coef[None], axis=1)
        if mode == "diff":
            hw = 2 * HEAD_DIM
            o4 = jnp.concatenate(
                [_rms(o4[:, h * hw:(h + 1) * hw], SUBLN_EPS) * gs_ref[...] for h in range(w // hw)],
                axis=1) * (1.0 - LAMBDA_INIT)
        o_ref[0] = o4.astype(BF16)


def _decode_page_map(b, j, pt, *, g, n_grp, n_pages, ndim):
    return (pt[b, n_pages - 1 - (j * n_grp + g)],) + (0,) * (ndim - 1)


def _decode_attention(page_table, q, k_new, v_new, extra, cache_kt, cache_v, cache_lft, mode):
    db, nq, w = q.shape
    n_pages = page_table.shape[1]
    n_grp = min(DECODE_PAGES, n_pages)
    seq = lambda a: pl.BlockSpec((1,) + a.shape[1:], lambda b, j, pt: (b,) + (0,) * (a.ndim - 1))
    small = lambda a: pl.BlockSpec(a.shape, lambda b, j, pt: (0,) * a.ndim)
    pages = lambda a: [pl.BlockSpec((1,) + a.shape[1:],
                                    functools.partial(_decode_page_map, g=g, n_grp=n_grp, n_pages=n_pages, ndim=a.ndim))
                       for g in range(n_grp)]
    in_specs = [seq(q), seq(k_new), seq(v_new)]
    if mode == "diff":
        in_specs += [small(a) for a in extra]
    else:
        in_specs += [seq(extra[0])]
    in_specs += pages(cache_kt) + pages(cache_v)
    args = [q, k_new, v_new, *extra] + [cache_kt] * n_grp + [cache_v] * n_grp
    nchunk = w // HEAD_DIM
    rows = nq * nchunk
    scratch = [pltpu.VMEM((rows, LANES), F32), pltpu.VMEM((rows, LANES), F32), pltpu.VMEM((rows, w), F32)]
    if mode == "fox":
        in_specs += pages(cache_lft)
        args += [cache_lft] * n_grp
        scratch.append(pltpu.VMEM((nchunk, LANES), F32))
    return pl.pallas_call(
        functools.partial(_decode_kernel, n_grp=n_grp, mode=mode),
        grid_spec=pltpu.PrefetchScalarGridSpec(
            num_scalar_prefetch=1,
            grid=(db, n_pages // n_grp),
            in_specs=in_specs,
            out_specs=pl.BlockSpec((1, nq, w), lambda b, j, pt: (b, 0, 0)),
            scratch_shapes=scratch),
        out_shape=jax.ShapeDtypeStruct((db, nq, w), BF16),
        compiler_params=pltpu.CompilerParams(dimension_semantics=("arbitrary", "arbitrary"),
                                             vmem_limit_bytes=VMEM_LIMIT),
        name="sample_attention_" + mode,
    )(page_table, *args)


def _merge_kernel(oa_ref, ob_ref, ga_ref, gb_ref, x_ref, g1_ref, sh2_ref, sc2_ref,
                  wa_ref, wb_ref, wo_ref, gpost_ref, gpre_ref, wr_hi_ref, wr_lo_ref, br_ref, cnt_in_ref,
                  *rest, n_alias):
    x1_ref, h2_ref, idx_ref, w_ref, rank_ref, cnt_ref, carry_s = rest[n_alias:]
    i = pl.program_id(0)

    @pl.when(i == 0)
    def _():
        carry_s[...] = cnt_in_ref[...]

    d = x_ref.shape[-1]
    y = ga_ref[...] * _dot(oa_ref[...], wa_ref[...]) + gb_ref[...] * _dot(ob_ref[...], wb_ref[...])
    y = _dot(y.astype(BF16), wo_ref[...])
    x1 = x_ref[...] + g1_ref[...].reshape(-1, d) * (_rms(y, NORM_EPS) * gpost_ref[...])
    x1_ref[...] = x1
    h = _rms(x1, NORM_EPS) * gpre_ref[...]
    h = h * (1.0 + sc2_ref[...].reshape(-1, d)) + sh2_ref[...].reshape(-1, d)
    h2_ref[...] = h

    h_hi, h_lo = _split_bf16(h)
    nt = lambda a, b: lax.dot_general(a, b, NT_DIMS, preferred_element_type=F32)
    logit = nt(wr_hi_ref[...], h_hi) + nt(wr_hi_ref[...], h_lo) + nt(wr_lo_ref[...], h_hi) + br_ref[...]
    ne, tm = logit.shape
    e_iota = lax.broadcasted_iota(jnp.int32, (ne, tm), 0)
    idxs, vals = [], []
    for _ in range(TOP_K):
        mx = jnp.max(logit, axis=0, keepdims=True)
        ik = jnp.min(jnp.where(logit == mx, e_iota, ne), axis=0, keepdims=True)
        idxs.append(ik)
        vals.append(mx)
        logit = jnp.where(e_iota == ik, -jnp.inf, logit)
    ex = [jnp.exp(v - vals[0]) for v in vals]
    den = ex[0]
    for e in ex[1:]:
        den = den + e
    onehot = jnp.zeros((ne, tm), F32)
    for ik in idxs:
        onehot = onehot + (e_iota == ik).astype(F32)
    tr = lax.broadcasted_iota(jnp.int32, (tm, tm), 0)
    tc = lax.broadcasted_iota(jnp.int32, (tm, tm), 1)
    before = _dot(onehot.astype(BF16), (tr < tc).astype(BF16)) + carry_s[...]
    for k in range(TOP_K):
        idx_ref[k:k + 1, :] = idxs[k]
        w_ref[k:k + 1, :] = ex[k] / den
        rank_ref[k:k + 1, :] = jnp.sum(jnp.where(e_iota == idxs[k], before, 0.0), axis=0, keepdims=True).astype(jnp.int32)
    carry_s[...] = carry_s[...] + jnp.sum(onehot, axis=1, keepdims=True)
    cnt_ref[...] = carry_s[...]


def _merge_route(oa, ob, gates, x2d, g1, sh2, sc2, wa, wb, wo, gpost, gpre, wr_hi, wr_lo, br, cnt_in,
                 n_total, row_offset, prev, rows_per_seq):
    n, d = x2d.shape
    tm = min(TOKEN_TILE, n)
    off = row_offset // tm
    if g1.ndim == 3:
        tps = rows_per_seq // tm
        mod_spec = pl.BlockSpec((1, 1, d), lambda i: (i // tps, 0, 0))
    else:
        mod_spec = pl.BlockSpec((tm, d), lambda i: (i, 0))
    const = lambda a: pl.BlockSpec(a.shape, lambda i: (0, 0), pipeline_mode=pl.Buffered(1))
    row = lambda w, c=0: pl.BlockSpec((tm, w), lambda i: (i, c))
    anyspec = pl.BlockSpec(memory_space=pl.ANY)
    n_alias = len(prev)
    out_shape = [jax.ShapeDtypeStruct((n_total, d), F32), jax.ShapeDtypeStruct((n_total, d), F32),
                 jax.ShapeDtypeStruct((TOP_K, n_total), jnp.int32), jax.ShapeDtypeStruct((TOP_K, n_total), F32),
                 jax.ShapeDtypeStruct((TOP_K, n_total), jnp.int32), jax.ShapeDtypeStruct((N_EXPERTS, 1), F32)]
    out_specs = [pl.BlockSpec((tm, d), lambda i: (i + off, 0)), pl.BlockSpec((tm, d), lambda i: (i + off, 0)),
                 pl.BlockSpec((TOP_K, tm), lambda i: (0, i + off)), pl.BlockSpec((TOP_K, tm), lambda i: (0, i + off)),
                 pl.BlockSpec((TOP_K, tm), lambda i: (0, i + off)), pl.BlockSpec((N_EXPERTS, 1), lambda i: (0, 0))]
    n_in = 17
    return pl.pallas_call(
        functools.partial(_merge_kernel, n_alias=n_alias),
        grid=(n // tm,),
        in_specs=[row(oa.shape[1]), row(ob.shape[1]), row(d, 0), row(d, 1), row(d), mod_spec, mod_spec, mod_spec,
                  const(wa), const(wb), const(wo), const(gpost), const(gpre), const(wr_hi), const(wr_lo), const(br),
                  const(cnt_in)] + [anyspec] * n_alias,
        out_specs=out_specs,
        out_shape=out_shape,
        input_output_aliases={n_in + a: a for a in range(n_alias)},
        scratch_shapes=[pltpu.VMEM((N_EXPERTS, 1), F32)],
        compiler_params=pltpu.CompilerParams(dimension_semantics=("arbitrary",), vmem_limit_bytes=VMEM_LIMIT),
        name="merge_and_route",
    )(oa, ob, gates, gates, x2d, g1, sh2, sc2, wa, wb, wo, gpost, gpre, wr_hi, wr_lo, br, cnt_in, *prev)


def _dest_kernel(idx_ref, rank_ref, start_ref, o_ref):
    idx = idx_ref[...]
    e_iota = lax.broadcasted_iota(jnp.int32, (N_EXPERTS, idx.shape[1]), 0)
    for k in range(TOP_K):
        base = jnp.sum(jnp.where(e_iota == idx[k:k + 1], start_ref[...], 0), axis=0, keepdims=True)
        o_ref[k:k + 1, :] = base + rank_ref[k:k + 1, :]


def _dest_slots(idx, rank, pad_start):
    n = idx.shape[1]
    tn = min(2048, n)
    while n % tn:
        tn //= 2
    spec = pl.BlockSpec((TOP_K, tn), lambda i: (0, i))
    return pl.pallas_call(
        _dest_kernel, grid=(n // tn,),
        in_specs=[spec, spec, pl.BlockSpec((N_EXPERTS, 1), lambda i: (0, 0))],
        out_specs=spec, out_shape=jax.ShapeDtypeStruct((TOP_K, n), jnp.int32),
        name="moe_slots",
    )(idx, rank, pad_start)


def _dispatch_kernel(dest_ref, h_ref, xd_ref, sem):
    tm = h_ref.shape[0]

    def body(t, carry):
        for k in range(TOP_K):
            pltpu.make_async_copy(h_ref.at[pl.ds(t, 1), :], xd_ref.at[pl.ds(dest_ref[k, t], 1), :], sem).start()
        return carry

    lax.fori_loop(0, tm, body, 0)
    for _ in range(TOP_K):
        pltpu.make_async_copy(h_ref, xd_ref.at[pl.ds(0, tm), :], sem).wait()


def _dispatch(dest, h2, n_slots):
    n, d = h2.shape
    tm = min(TOKEN_TILE, n)
    while n % tm:
        tm //= 2
    return pl.pallas_call(
        _dispatch_kernel, grid=(n // tm,),
        in_specs=[pl.BlockSpec((TOP_K, tm), lambda i: (0, i), memory_space=pltpu.SMEM),
                  pl.BlockSpec((tm, d), lambda i: (i, 0))],
        out_specs=pl.BlockSpec(memory_space=pl.ANY),
        out_shape=jax.ShapeDtypeStruct((n_slots, d), F32),
        scratch_shapes=[pltpu.SemaphoreType.DMA(())],
        compiler_params=pltpu.CompilerParams(dimension_semantics=("arbitrary",)),
        name="moe_dispatch",
    )(dest, h2)


def _expert_kernel(be_ref, nv_ref, nu_ref, x_ref, w1_ref, b1_ref, w2_ref, b2_ref, y_ref):
    del be_ref
    i = pl.program_id(0)

    @pl.when(i < nu_ref[0])
    def _():
        x = x_ref[...]
        row = lax.broadcasted_iota(jnp.int32, (x.shape[0], 1), 0)
        x = jnp.where(row < nv_ref[i], x, 0.0).astype(BF16)
        hh = _dot(x, w1_ref[0]) + b1_ref[0]
        f = hh.shape[1] // 2
        x_glu = jnp.minimum(hh[:, :f], SWIGLU_LIMIT)
        x_lin = jnp.clip(hh[:, f:], -SWIGLU_LIMIT, SWIGLU_LIMIT)
        act = x_glu * jax.nn.sigmoid(SWIGLU_ALPHA * x_glu) * (x_lin + 1.0)
        y_ref[...] = _dot(act.astype(BF16), w2_ref[0]) + b2_ref[0]


def _experts(block_expert, n_valid, n_used, x_disp, w1, b1, w2, b2):
    n_slots, d = x_disp.shape
    tb = EXPERT_TILE
    ne, _, f2 = w1.shape
    blk = lambda i, be, nv, nu: (jnp.minimum(i, nu[0] - 1), 0)
    wsel = lambda i, be, nv, nu: (be[i], 0, 0)
    return pl.pallas_call(
        _expert_kernel,
        grid_spec=pltpu.PrefetchScalarGridSpec(
            num_scalar_prefetch=3,
            grid=(n_slots // tb,),
            in_specs=[pl.BlockSpec((tb, d), blk),
                      pl.BlockSpec((1, d, f2), wsel), pl.BlockSpec((1, 1, f2), wsel),
                      pl.BlockSpec((1, f2 // 2, d), wsel), pl.BlockSpec((1, 1, d), wsel)],
            out_specs=pl.BlockSpec((tb, d), blk)),
        out_shape=jax.ShapeDtypeStruct((n_slots, d), F32),
        compiler_params=pltpu.CompilerParams(dimension_semantics=("arbitrary",), vmem_limit_bytes=VMEM_LIMIT),
        name="moe_experts",
    )(block_expert, n_valid, n_used, x_disp, w1, b1.reshape(ne, 1, f2), w2, b2.reshape(ne, 1, d))


def _combine_kernel(dest_ref, w_ref, x1_ref, g2_ref, gpost_ref, yd_ref, o_ref, buf, sem):
    tm, d = x1_ref.shape

    def body(t, carry):
        for k in range(TOP_K):
            pltpu.make_async_copy(yd_ref.at[pl.ds(dest_ref[k, t], 1), :], buf.at[k, pl.ds(t, 1), :], sem).start()
        return carry

    lax.fori_loop(0, tm, body, 0)
    for k in range(TOP_K):
        pltpu.make_async_copy(yd_ref.at[pl.ds(0, tm), :], buf.at[k], sem).wait()
    wt = w_ref[...].T
    y = wt[:, 0:1] * buf[0]
    for k in range(1, TOP_K):
        y = y + wt[:, k:k + 1] * buf[k]
    o_ref[...] = x1_ref[...] + g2_ref[...].reshape(-1, d) * (_rms(y, NORM_EPS) * gpost_ref[...])


def _combine(dest, w, x1, g2, gpost, y_disp, n, row_offset, rows_per_seq):
    d = x1.shape[1]
    tm = min(TOKEN_TILE, n)
    off = row_offset // tm
    if g2.ndim == 3:
        tps = rows_per_seq // tm
        mod_spec = pl.BlockSpec((1, 1, d), lambda i: (i // tps, 0, 0))
    else:
        mod_spec = pl.BlockSpec((tm, d), lambda i: (i, 0))
    return pl.pallas_call(
        _combine_kernel, grid=(n // tm,),
        in_specs=[pl.BlockSpec((TOP_K, tm), lambda i: (0, i + off), memory_space=pltpu.SMEM),
                  pl.BlockSpec((TOP_K, tm), lambda i: (0, i + off)),
                  pl.BlockSpec((tm, d), lambda i: (i + off, 0)),
                  mod_spec,
                  pl.BlockSpec((1, d), lambda i: (0, 0)),
                  pl.BlockSpec(memory_space=pl.ANY)],
        out_specs=pl.BlockSpec((tm, d), lambda i: (i, 0)),
        out_shape=jax.ShapeDtypeStruct((n, d), F32),
        scratch_shapes=[pltpu.VMEM((TOP_K, tm, d), F32), pltpu.SemaphoreType.DMA(())],
        compiler_params=pltpu.CompilerParams(dimension_semantics=("arbitrary",), vmem_limit_bytes=VMEM_LIMIT),
        name="moe_combine",
    )(dest, w, x1, g2, gpost, y_disp)


def kernel(x_prompt, x_sample, c_prompt, c_sample, cache_a_k, cache_a_v, cache_b_k, cache_b_v, cache_b_logf, page_table, w_mod, b_mod, g_pre_mix, g_post_mix, g_pre_ffn, g_post_ffn, w_in, b_forget, b_gate, lambda_q1, lambda_k1, lambda_q2, lambda_k2, g_subln, w_br_a, w_br_b, w_out, w_router, b_router, w_mlp1, b_mlp1, w_mlp2, b_mlp2):
    bsz, seq, d = x_prompt.shape
    db, nq, _ = x_sample.shape
    n_pool, page = cache_a_k.shape[1], cache_a_k.shape[2]
    n_pages = page_table.shape[1]
    past_len = n_pages * page
    a_heads = cache_a_k.shape[3]
    b_heads = cache_b_k.shape[3]
    n_p, n_s = bsz * seq, db * nq
    n_tot = n_p + n_s
    w = QKV_WIDTH

    w_in0 = w_in[0]
    wqkv = w_in0[:, :6 * w].astype(BF16)
    wf_cols = w_in0[:, 6 * w:6 * w + b_heads]
    wf = jnp.pad(wf_cols, ((0, 0), (0, LANES - b_heads))).astype(BF16)
    wg = w_in0[:, 6 * w + b_heads:].astype(BF16)
    bf = jnp.pad(b_forget[0], (0, LANES - b_heads)).reshape(1, LANES)
    bg = b_gate[0].reshape(1, -1)
    wq = jnp.concatenate([wqkv[:, 0:w], wqkv[:, 3 * w:4 * w]], axis=1)
    wv = wqkv[:, 2 * w:3 * w]
    wt = jnp.concatenate([wqkv[:, w:2 * w], wqkv[:, 4 * w:5 * w], wqkv[:, 5 * w:6 * w]], axis=1).T
    wft = jnp.pad(wf_cols.T, ((0, 16 - b_heads), (0, 0))).astype(BF16)
    bft = b_forget[0].reshape(b_heads, 1)
    row = lambda a: a[0].reshape(1, -1)
    lam_args = (row(lambda_q1), row(lambda_k1), row(lambda_q2), row(lambda_k2), row(g_subln))
    wr_t = w_router[0].T
    wr_hi = wr_t.astype(BF16)
    wr_lo = (wr_t - wr_hi.astype(F32)).astype(BF16)
    br = b_router[0].reshape(N_EXPERTS, 1)

    mod = _adaln(jnp.concatenate([c_prompt, c_sample], axis=0), w_mod[0], b_mod[0]).reshape(bsz + db, 6, d)
    mod_p = [mod[:bsz, i].reshape(bsz, 1, d) for i in range(6)]
    mod_s = [jnp.repeat(mod[bsz:, i], nq, axis=0) for i in range(6)]

    pos_p = jnp.arange(seq, dtype=F32)
    ang_t = _rope_angles(pos_p).T
    tabs_s = tuple(jnp.tile(t, (db, 1)) for t in _rope_tables(past_len + jnp.arange(nq, dtype=F32)))
    xp2 = x_prompt.reshape(n_p, d)
    xs2 = x_sample.reshape(n_s, d)
    qa, kat, va, vab, qb, kbt, vbt, lft, gates = _project_prompt(
        xp2, mod_p[0], mod_p[1], row(g_pre_mix), _rope_tables(pos_p), (jnp.cos(ang_t), jnp.sin(ang_t)),
        wq, wv, wt, wft, wg, bft, bg, bsz, seq, a_heads, b_heads)
    qa_s, ka_s, va_s, qb_s, kb_s, vb_s, lf_s, gates_s = _project_sample(
        xs2, mod_s[0], mod_s[1], row(g_pre_mix), tabs_s, wqkv, wf, wg, bf, bg, b_heads)

    sh3 = lambda a: a.reshape(bsz, seq, w)
    oa = _prompt_attention(sh3(qa), kat, sh3(vab), lam_args, "diff")
    ob = _prompt_attention(sh3(qb), kbt, vbt, _cum_logf(lft), "fox")

    sd3 = lambda a: a.reshape(db, nq, -1)
    kt_a = jnp.transpose(cache_a_k[0], (0, 2, 3, 4, 1)).reshape(n_pool, w, page)
    kt_b = jnp.transpose(cache_b_k[0], (0, 2, 3, 1)).reshape(n_pool, w, page)
    vt_b = jnp.transpose(cache_b_v[0], (0, 2, 3, 1)).reshape(n_pool, w, page)
    lft_b = jnp.transpose(cache_b_logf[0], (0, 2, 1))
    oa_s = _decode_attention(page_table, sd3(qa_s), sd3(ka_s), sd3(va_s), lam_args,
                             kt_a, cache_a_v[0].reshape(n_pool, page * a_heads, 2 * HEAD_DIM), None, "diff")
    ob_s = _decode_attention(page_table, sd3(qb_s), sd3(kb_s), sd3(vb_s), (sd3(lf_s),),
                             kt_b, vt_b, lft_b, "fox")

    wa, wb, wo = w_br_a[0].astype(BF16), w_br_b[0].astype(BF16), w_out[0].astype(BF16)
    common = (wa, wb, wo, row(g_post_mix), row(g_pre_ffn), wr_hi, wr_lo, br)
    outs_p = _merge_route(oa.reshape(n_p, w), ob.reshape(n_p, w), gates, xp2, mod_p[2], mod_p[3], mod_p[4], *common,
                          jnp.zeros((N_EXPERTS, 1), F32), n_tot, 0, (), seq)
    x1, h2, idx, wts, rank, counts = _merge_route(
        oa_s.reshape(n_s, w), ob_s.reshape(n_s, w), gates_s, xs2, mod_s[2], mod_s[3], mod_s[4], *common,
        outs_p[5], n_tot, n_p, tuple(outs_p[:5]), nq)

    tb = EXPERT_TILE
    n_blocks = (n_tot * TOP_K + N_EXPERTS * (tb - 1)) // tb
    cnt = counts[:, 0].astype(jnp.int32)
    padded = (cnt + tb - 1) // tb * tb
    pad_end = jnp.cumsum(padded)
    pad_start = pad_end - padded
    blk_start = jnp.arange(n_blocks, dtype=jnp.int32) * tb
    block_expert = jnp.minimum(jnp.searchsorted(pad_end, blk_start, side="right"), N_EXPERTS - 1).astype(jnp.int32)
    n_valid = jnp.clip(cnt[block_expert] - (blk_start - pad_start[block_expert]), 0, tb).astype(jnp.int32)
    n_used = (pad_end[-1:] // tb).astype(jnp.int32)

    dest = _dest_slots(idx, rank, pad_start.reshape(N_EXPERTS, 1).astype(jnp.int32))
    x_disp = _dispatch(dest, h2, n_blocks * tb)
    y_disp = _experts(block_expert, n_valid, n_used, x_disp,
                      w_mlp1[0].astype(BF16), b_mlp1[0], w_mlp2[0].astype(BF16), b_mlp2[0])
    y_p = _combine(dest, wts, x1, mod_p[5], row(g_post_ffn), y_disp, n_p, 0, seq)
    y_s = _combine(dest, wts, x1, mod_s[5], row(g_post_ffn), y_disp, n_s, n_p, nq)

    new_ka = jnp.transpose(kat.reshape(1, bsz, a_heads, 2, HEAD_DIM, seq), (0, 1, 5, 2, 3, 4))
    new_kb = jnp.transpose(kbt.reshape(1, bsz, b_heads, HEAD_DIM, seq), (0, 1, 4, 2, 3))
    new_vb = jnp.transpose(vbt.reshape(1, bsz, b_heads, HEAD_DIM, seq), (0, 1, 4, 2, 3))
    new_lf = jnp.transpose(lft.reshape(1, bsz, b_heads, seq), (0, 1, 3, 2))
    return (y_p.reshape(bsz, seq, d), y_s.reshape(db, nq, d),
            new_ka, va.reshape(1, bsz, seq, a_heads, 2 * HEAD_DIM), new_kb, new_vb, new_lf,
            ka_s.reshape(1, db, nq, a_heads, 2, HEAD_DIM), va_s.reshape(1, db, nq, a_heads, 2 * HEAD_DIM),
            kb_s.reshape(1, db, nq, b_heads, HEAD_DIM), vb_s.reshape(1, db, nq, b_heads, HEAD_DIM),
            lf_s.reshape(1, db, nq, b_heads))
```

```python
import functools
import math

import jax
import jax.numpy as jnp
from jax import lax
from jax.experimental import pallas as pl
from jax.experimental.pallas import tpu as pltpu

F32 = jnp.float32
BF16 = jnp.bfloat16

HEAD_DIM = 64
ROT_DIM = HEAD_DIM // 4
ROPE_THETA = 500000.0
ATTN_SCALE = HEAD_DIM ** -0.5
N_EXPERTS = 32
TOP_K = 4
SWIGLU_LIMIT = 7.0
SWIGLU_ALPHA = 1.702
NORM_EPS = 1e-6
SUBLN_EPS = 1e-5
LAMBDA_INIT = 0.8 - 0.6

LANES = 128
QKV_WIDTH = 512
TOKEN_TILE = 256
ATTN_TILE = 512
KV_TILES = 2
EXPERT_TILE = 512
RUN_ALIGN = 8
DECODE_PAGES = 16
VMEM_LIMIT = 56 * 1024 * 1024

NT_DIMS = (((1,), (1,)), ((), ()))


def _dot(a, b):
    return jnp.dot(a, b, preferred_element_type=F32)


def _split_bf16(x):
    hi = x.astype(BF16)
    lo = (x - hi.astype(F32)).astype(BF16)
    return hi, lo


def _rms(x, eps):
    return x * lax.rsqrt(jnp.mean(x * x, axis=-1, keepdims=True) + eps)


def _lam(lq1, lk1, lq2, lk2):
    a = jnp.exp(jnp.sum(lq1[...] * lk1[...], axis=1, keepdims=True))
    b = jnp.exp(jnp.sum(lq2[...] * lk2[...], axis=1, keepdims=True))
    return a - b + LAMBDA_INIT


def _mod_kernel(c_ref, w_ref, b_ref, o_ref):
    c = c_ref[...]
    s_hi, s_lo = _split_bf16(c * jax.nn.sigmoid(c))
    w_hi, w_lo = _split_bf16(w_ref[...])
    o_ref[0] = _dot(s_hi, w_hi) + _dot(s_lo, w_hi) + _dot(s_hi, w_lo) + b_ref[...]


def _adaln(c, w_mod, b_mod):
    n, d = c.shape
    width = w_mod.shape[1]
    return pl.pallas_call(
        _mod_kernel,
        grid=(width // d,),
        in_specs=[pl.BlockSpec((n, d), lambda j: (0, 0)),
                  pl.BlockSpec((d, d), lambda j: (0, j)),
                  pl.BlockSpec((1, d), lambda j: (0, j))],
        out_specs=pl.BlockSpec((1, n, d), lambda j: (j, 0, 0)),
        out_shape=jax.ShapeDtypeStruct((width // d, n, d), F32),
        name="adaln_mod",
    )(c, w_mod, b_mod.reshape(1, width))


def _normed_input(x_ref, shift_ref, scale_ref, g_ref):
    x = x_ref[...]
    d = x.shape[-1]
    h = _rms(x, NORM_EPS) * g_ref[...]
    return (h * (1.0 + scale_ref[...].reshape(-1, d)) + shift_ref[...].reshape(-1, d)).astype(BF16)


def _rope_lanes(y, cos, sa, sb):
    parts = []
    for g in range(y.shape[1] // LANES):
        yg = y[:, g * LANES:(g + 1) * LANES]
        parts.append(yg * cos + pltpu.roll(yg, ROT_DIM // 2, 1) * sa
                     + pltpu.roll(yg, LANES - ROT_DIM // 2, 1) * sb)
    return jnp.concatenate(parts, axis=1)


def _log_sigmoid(z):
    return jnp.minimum(z, 0.0) - jnp.log1p(jnp.exp(-jnp.abs(z)))


def _proj_kernel(x_ref, shift_ref, scale_ref, g_ref, cos_ref, sa_ref, sb_ref,
                 wqkv_ref, wf_ref, wg_ref, bf_ref, bg_ref,
                 qa_ref, ka_ref, va_ref, qb_ref, kb_ref, vb_ref, lf_ref, gate_ref):
    hb = _normed_input(x_ref, shift_ref, scale_ref, g_ref)
    cos, sa, sb = cos_ref[...], sa_ref[...], sb_ref[...]

    def seg(i):
        return _dot(hb, wqkv_ref[:, i * QKV_WIDTH:(i + 1) * QKV_WIDTH])

    qa_ref[...] = (_rope_lanes(seg(0), cos, sa, sb) * ATTN_SCALE).astype(BF16)
    ka_ref[...] = _rope_lanes(seg(1), cos, sa, sb)
    va_ref[...] = seg(2)
    qb_ref[...] = (seg(3) * ATTN_SCALE).astype(BF16)
    kb_ref[...] = seg(4)
    vb_ref[...] = seg(5)
    z = (_dot(hb, wf_ref[...]) + bf_ref[...])[:, :lf_ref.shape[-1]]
    lf_ref[...] = _log_sigmoid(z)
    gate_ref[...] = jax.nn.sigmoid(_dot(hb, wg_ref[...]) + bg_ref[...]).astype(BF16)


def _project_sample(x2d, shift, scale, g_pre, tabs, wqkv, wf, wg, bf, bg, n_heads_b):
    n, d = x2d.shape
    tm = min(TOKEN_TILE, n)
    const = lambda a: pl.BlockSpec(a.shape, lambda i: (0, 0), pipeline_mode=pl.Buffered(1))
    row = lambda w: pl.BlockSpec((tm, w), lambda i: (i, 0))
    gw = wg.shape[1]
    return pl.pallas_call(
        _proj_kernel,
        grid=(n // tm,),
        in_specs=[row(d), row(d), row(d), const(g_pre), row(LANES), row(LANES), row(LANES),
                  const(wqkv), const(wf), const(wg), const(bf), const(bg)],
        out_specs=[row(QKV_WIDTH)] * 6 + [row(n_heads_b), row(gw)],
        out_shape=[jax.ShapeDtypeStruct((n, QKV_WIDTH), BF16),
                   jax.ShapeDtypeStruct((n, QKV_WIDTH), F32),
                   jax.ShapeDtypeStruct((n, QKV_WIDTH), F32),
                   jax.ShapeDtypeStruct((n, QKV_WIDTH), BF16),
                   jax.ShapeDtypeStruct((n, QKV_WIDTH), F32),
                   jax.ShapeDtypeStruct((n, QKV_WIDTH), F32),
                   jax.ShapeDtypeStruct((n, n_heads_b), F32),
                   jax.ShapeDtypeStruct((n, gw), BF16)],
        compiler_params=pltpu.CompilerParams(dimension_semantics=("arbitrary",), vmem_limit_bytes=VMEM_LIMIT),
        name="sample_projection",
    )(x2d, shift, scale, g_pre, *tabs, wqkv, wf, wg, bf, bg)


def _proj_prompt_kernel(x_ref, shift_ref, scale_ref, g_ref, cos_ref, sa_ref, sb_ref, cost_ref, sint_ref,
                        wq_ref, wv_ref, wt_ref, wft_ref, wg_ref, bft_ref, bg_ref,
                        qa_ref, kat_ref, va_ref, vab_ref, qb_ref, kbt_ref, vbt_ref, lft_ref, gate_ref):
    hb = _normed_input(x_ref, shift_ref, scale_ref, g_ref)
    hbt = hb.T
    w = QKV_WIDTH
    qa = _rope_lanes(_dot(hb, wq_ref[:, :w]), cos_ref[...], sa_ref[...], sb_ref[...])
    qa_ref[...] = (qa * ATTN_SCALE).astype(BF16)
    qb_ref[...] = (_dot(hb, wq_ref[:, w:]) * ATTN_SCALE).astype(BF16)
    v = _dot(hb, wv_ref[...])
    vab_ref[...] = v.astype(BF16)
    for h in range(va_ref.shape[1]):
        va_ref[:, h, :] = v[:, h * LANES:(h + 1) * LANES]

    kat = _dot(wt_ref[0:w, :], hbt)
    cos_t, sin_t = cost_ref[...], sint_ref[...]
    half = ROT_DIM // 2
    parts = []
    for c in range(w // HEAD_DIM):
        base = c * HEAD_DIM
        x1 = kat[base:base + half]
        x2 = kat[base + half:base + ROT_DIM]
        parts += [x1 * cos_t - x2 * sin_t, x2 * cos_t + x1 * sin_t, kat[base + ROT_DIM:base + HEAD_DIM]]
    kat_ref[0] = jnp.concatenate(parts, axis=0)
    kbt_ref[0] = _dot(wt_ref[w:2 * w, :], hbt)
    vbt_ref[0] = _dot(wt_ref[2 * w:3 * w, :], hbt)
    nh = lft_ref.shape[1]
    lft_ref[0] = _log_sigmoid(_dot(wft_ref[...], hbt)[:nh] + bft_ref[...])
    gate_ref[...] = jax.nn.sigmoid(_dot(hb, wg_ref[...]) + bg_ref[...]).astype(BF16)


def _project_prompt(x2d, shift, scale, g_pre, tabs, tabs_t, wq, wv, wt, wft, wg, bft, bg, bsz, seq, a_heads, b_heads):
    n, d = x2d.shape
    tm = min(TOKEN_TILE, seq)
    tps = seq // tm
    w = QKV_WIDTH
    const = lambda a: pl.BlockSpec(a.shape, lambda i: (0, 0), pipeline_mode=pl.Buffered(1))
    row = lambda width: pl.BlockSpec((tm, width), lambda i: (i, 0))
    mod_spec = pl.BlockSpec((1, 1, d), lambda i: (i // tps, 0, 0))
    tab_spec = pl.BlockSpec((tm, LANES), lambda i: (i % tps, 0))
    tabt_spec = pl.BlockSpec((ROT_DIM // 2, tm), lambda i: (0, i % tps))
    tr_spec = lambda rows: pl.BlockSpec((1, rows, tm), lambda i: (i // tps, 0, i % tps))
    gw = wg.shape[1]
    return pl.pallas_call(
        _proj_prompt_kernel,
        grid=(n // tm,),
        in_specs=[row(d), mod_spec, mod_spec, const(g_pre), tab_spec, tab_spec, tab_spec, tabt_spec, tabt_spec,
                  const(wq), const(wv), const(wt), const(wft), const(wg), const(bft), const(bg)],
        out_specs=[row(w), tr_spec(w), pl.BlockSpec((tm, a_heads, LANES), lambda i: (i, 0, 0)), row(w),
                   row(w), tr_spec(w), tr_spec(w), tr_spec(b_heads), row(gw)],
        out_shape=[jax.ShapeDtypeStruct((n, w), BF16),
                   jax.ShapeDtypeStruct((bsz, w, seq), F32),
                   jax.ShapeDtypeStruct((n, a_heads, LANES), F32),
                   jax.ShapeDtypeStruct((n, w), BF16),
                   jax.ShapeDtypeStruct((n, w), BF16),
                   jax.ShapeDtypeStruct((bsz, w, seq), F32),
                   jax.ShapeDtypeStruct((bsz, w, seq), F32),
                   jax.ShapeDtypeStruct((bsz, b_heads, seq), F32),
                   jax.ShapeDtypeStruct((n, gw), BF16)],
        compiler_params=pltpu.CompilerParams(dimension_semantics=("arbitrary",), vmem_limit_bytes=VMEM_LIMIT),
        name="prompt_projection",
    )(x2d, shift, scale, g_pre, *tabs, *tabs_t, wq, wv, wt, wft, wg, bft, bg)


def _rope_angles(pos):
    half = ROT_DIM // 2
    inv_freq = jnp.power(ROPE_THETA, -jnp.arange(half, dtype=F32) * 2.0 / ROT_DIM)
    return pos[:, None] * inv_freq[None, :]


def _rope_tables(pos):
    half = ROT_DIM // 2
    ang = _rope_angles(pos)
    cos, sin = jnp.cos(ang), jnp.sin(ang)
    s = pos.shape[0]
    ones = jnp.ones((s, HEAD_DIM - ROT_DIM), F32)
    zeros = jnp.zeros((s, HEAD_DIM - ROT_DIM), F32)
    zh = jnp.zeros((s, half), F32)
    cos_t = jnp.concatenate([cos, cos, ones], axis=1)
    sa_t = jnp.concatenate([zh, sin, zeros], axis=1)
    sb_t = jnp.concatenate([-sin, zh, zeros], axis=1)
    rep = LANES // HEAD_DIM
    return tuple(jnp.tile(t, (1, rep)) for t in (cos_t, sa_t, sb_t))


def _cum_kernel(lft_ref, c_ref, ct_ref, *, chunk):
    nh, s = lft_ref.shape[1], lft_ref.shape[2]
    r = lax.broadcasted_iota(jnp.int32, (chunk, chunk), 0)
    c = lax.broadcasted_iota(jnp.int32, (chunk, chunk), 1)
    upto = (r <= c).astype(F32)
    carry = jnp.zeros((nh, 1), F32)
    for i in range(s // chunk):
        x = lft_ref[0, :, i * chunk:(i + 1) * chunk]
        cum = jnp.dot(x, upto, preferred_element_type=F32, precision=lax.Precision.HIGHEST) + carry
        carry = cum[:, chunk - 1:chunk]
        cum_n = cum.T
        for p in range(nh // 2):
            ct_ref[0, p, :, i * chunk:(i + 1) * chunk] = cum[2 * p:2 * p + 2, :]
            c_ref[0, p, i * chunk:(i + 1) * chunk, :] = cum_n[:, 2 * p:2 * p + 2]


def _cum_logf(lft):
    b, nh, s = lft.shape
    chunk = min(256, s)
    return pl.pallas_call(
        functools.partial(_cum_kernel, chunk=chunk),
        grid=(b,),
        in_specs=[pl.BlockSpec((1, nh, s), lambda i: (i, 0, 0))],
        out_specs=[pl.BlockSpec((1, nh // 2, s, 2), lambda i: (i, 0, 0, 0)),
                   pl.BlockSpec((1, nh // 2, 2, s), lambda i: (i, 0, 0, 0))],
        out_shape=[jax.ShapeDtypeStruct((b, nh // 2, s, 2), F32),
                   jax.ShapeDtypeStruct((b, nh // 2, 2, s), F32)],
        name="forget_cumsum",
    )(lft)


def _pattn_kernel(*refs, tq, mode):
    if mode == "diff":
        q_ref, kt_ref, v_ref, lq1, lk1, lq2, lk2, gs_ref, o_ref, kt_s, v_s, m_s, acc_s = refs
    else:
        q_ref, kt_ref, vt_ref, c_ref, ct_ref, o_ref, kt_s, v_s, m_s, acc_s = refs
    qi = pl.program_id(2)
    tk = KV_TILES * tq

    @pl.when(qi == 0)
    def _():
        kt_s[...] = kt_ref[0].astype(BF16)
        if mode == "diff":
            v_s[:, :LANES] = v_ref[0]
            v_s[:, LANES:] = jnp.ones((v_s.shape[0], LANES), BF16)
        else:
            v = vt_ref[0].T
            first = lax.broadcasted_iota(jnp.int32, v.shape, 1) < HEAD_DIM
            v_s[0] = jnp.where(first, v, 1.0).astype(BF16)
            v_s[1] = jnp.where(first, 1.0, v).astype(BF16)

    q = q_ref[0]
    lane = lax.broadcasted_iota(jnp.int32, q.shape, 1)
    zero = jnp.zeros_like(q)
    qz = (jnp.where(lane < HEAD_DIM, q, zero), jnp.where(lane >= HEAD_DIM, q, zero))
    m_s[...] = jnp.full(m_s.shape, -jnp.inf, F32)
    acc_s[...] = jnp.zeros(acc_s.shape, F32)
    if mode == "fox":
        cb = c_ref[0, 0]
        rowb = tuple(jnp.broadcast_to(cb[:, hf:hf + 1], (tq, LANES)) for hf in range(2))
    row = lax.broadcasted_iota(jnp.int32, (tq, LANES), 0)
    col = lax.broadcasted_iota(jnp.int32, (tq, LANES), 1)
    n_acc = acc_s.shape[2] // LANES

    def tile(off, width, masked):
        kt = kt_s[:, pl.ds(off, width)]
        for hf in range(2):
            vt = v_s[pl.ds(off, width), :] if mode == "diff" else v_s[hf, pl.ds(off, width), :]
            s = _dot(qz[hf], kt)
            chunks = []
            for c in range(width // LANES):
                sc = s[:, c * LANES:(c + 1) * LANES]
                if mode == "fox":
                    sc = sc + (rowb[hf] - ct_ref[0, 0, hf:hf + 1, pl.ds(off + c * LANES, LANES)])
                if masked:
                    sc = jnp.where(col + c * LANES <= row, sc, -jnp.inf)
                chunks.append(sc)
            mx = chunks[0]
            for sc in chunks[1:]:
                mx = jnp.maximum(mx, sc)
            m_old = m_s[hf]
            m_new = jnp.maximum(m_old, jnp.max(mx, axis=1, keepdims=True))
            alpha = jnp.exp(m_old - m_new)
            p = jnp.concatenate([jnp.exp(sc - m_new) for sc in chunks], axis=1).astype(BF16)
            acc_s[hf] = jnp.concatenate([alpha] * n_acc, axis=1) * acc_s[hf] + _dot(p, vt)
            m_s[hf] = m_new

    def body(j, carry):
        tile(pl.multiple_of(j * tk, tk), tk, False)
        return carry

    lax.fori_loop(0, qi // KV_TILES, body, 0)
    if KV_TILES == 2:
        @pl.when(qi % 2 == 1)
        def _():
            tile(pl.multiple_of((qi - 1) * tq, tq), tq, False)

    tile(pl.multiple_of(qi * tq, tq), tq, True)

    if mode == "diff":
        a0, a1 = acc_s[0], acc_s[1]
        o = a0[:, :LANES] / a0[:, LANES:] - _lam(lq1, lk1, lq2, lk2) * (a1[:, :LANES] / a1[:, LANES:])
        o = _rms(o, SUBLN_EPS) * gs_ref[...] * (1.0 - LAMBDA_INIT)
    else:
        a0, a1 = acc_s[0], acc_s[1]
        o = jnp.where(lane < HEAD_DIM, a0 / pltpu.roll(a0, HEAD_DIM, 1), a1 / pltpu.roll(a1, HEAD_DIM, 1))
    o_ref[0] = o.astype(BF16)


def _prompt_attention(q, kt, v, extra, mode):
    b, s, w = q.shape
    tq = min(ATTN_TILE, s)
    nblk = w // LANES
    qspec = pl.BlockSpec((1, tq, LANES), lambda bi, h, qi: (bi, qi, h))
    ktspec = pl.BlockSpec((1, LANES, s), lambda bi, h, qi: (bi, h, 0))
    if mode == "diff":
        small = lambda a: pl.BlockSpec(a.shape, lambda bi, h, qi: (0, 0))
        in_specs = [qspec, ktspec, pl.BlockSpec((1, s, LANES), lambda bi, h, qi: (bi, 0, h))] + [small(a) for a in extra]
        scratch = [pltpu.VMEM((LANES, s), BF16), pltpu.VMEM((s, 2 * LANES), BF16),
                   pltpu.VMEM((2, tq, LANES), F32), pltpu.VMEM((2, tq, 2 * LANES), F32)]
    else:
        in_specs = [qspec, ktspec, ktspec,
                    pl.BlockSpec((1, 1, tq, 2), lambda bi, h, qi: (bi, h, qi, 0)),
                    pl.BlockSpec((1, 1, 2, s), lambda bi, h, qi: (bi, h, 0, 0))]
        scratch = [pltpu.VMEM((LANES, s), BF16), pltpu.VMEM((2, s, LANES), BF16),
                   pltpu.VMEM((2, tq, LANES), F32), pltpu.VMEM((2, tq, LANES), F32)]
    return pl.pallas_call(
        functools.partial(_pattn_kernel, tq=tq, mode=mode),
        grid=(b, nblk, s // tq),
        in_specs=in_specs,
        out_specs=qspec,
        out_shape=jax.ShapeDtypeStruct((b, s, w), BF16),
        scratch_shapes=scratch,
        compiler_params=pltpu.CompilerParams(dimension_semantics=("arbitrary", "arbitrary", "arbitrary"),
                                             vmem_limit_bytes=VMEM_LIMIT),
        name="prompt_attention_" + mode,
    )(q, kt, v, *extra)


def _decode_kernel(pt_ref, *refs, n_grp, mode):
    del pt_ref
    q_ref, kn_ref, vn_ref = refs[:3]
    pos = 3
    if mode == "diff":
        lq1, lk1, lq2, lk2, gs_ref = refs[pos:pos + 5]
        pos += 5
    else:
        lfn_ref = refs[pos]
        pos += 1
    kp = refs[pos:pos + n_grp]
    vp = refs[pos + n_grp:pos + 2 * n_grp]
    pos += 2 * n_grp
    if mode == "fox":
        lfp = refs[pos:pos + n_grp]
        pos += n_grp
    o_ref, m_s, l_s, acc_s = refs[pos:pos + 4]
    if mode == "fox":
        carry_s = refs[pos + 4]
    j = pl.program_id(1)
    nq, w = q_ref.shape[1], q_ref.shape[2]
    nchunk = w // HEAD_DIM
    rows = nq * nchunk
    page = kp[0].shape[2]
    rep = w // LANES

    def rep_rows(x):
        return jnp.concatenate([jnp.broadcast_to(x[i:i + 1], (nchunk, x.shape[1])) for i in range(nq)], axis=0)

    rid = lax.broadcasted_iota(jnp.int32, (rows, w), 0)
    lid = lax.broadcasted_iota(jnp.int32, (rows, w), 1)
    qbd = jnp.where(lid // HEAD_DIM == rid % nchunk, rep_rows(q_ref[0].astype(F32)), 0.0)
    qbd_b = qbd.astype(BF16)
    if mode == "fox":
        lfn = lfn_ref[0]
        cn = [lfn[0:1]]
        for i in range(1, nq):
            cn.append(cn[-1] + lfn[i:i + 1])
        r8 = lax.broadcasted_iota(jnp.int32, (rows, nchunk), 0)
        l8 = lax.broadcasted_iota(jnp.int32, (rows, nchunk), 1)
        sel = l8 == r8 % nchunk
        pick = lambda x: jnp.sum(jnp.where(sel, x, 0.0), axis=1, keepdims=True)
        rowb = pick(jnp.concatenate([jnp.broadcast_to(c, (nchunk, nchunk)) for c in cn], axis=0))
        rowb_l = jnp.broadcast_to(rowb, (rows, LANES))

    @pl.when(j == 0)
    def _():
        kn, vn = kn_ref[0], vn_ref[0]
        qq = lax.broadcasted_iota(jnp.int32, (rows, 1), 0) // nchunk
        ss = []
        for jn in range(nq):
            s = jnp.sum(qbd * kn[jn:jn + 1], axis=1, keepdims=True)
            if mode == "fox":
                s = s + (rowb - pick(jnp.broadcast_to(cn[jn], (rows, nchunk))))
            ss.append(jnp.where(jn <= qq, s, -jnp.inf))
        m = ss[0]
        for s in ss[1:]:
            m = jnp.maximum(m, s)
        ps = [jnp.exp(s - m) for s in ss]
        l = ps[0]
        acc = ps[0] * vn[0:1]
        for jn in range(1, nq):
            l = l + ps[jn]
            acc = acc + ps[jn] * vn[jn:jn + 1]
        m_s[...] = jnp.broadcast_to(m, m_s.shape)
        l_s[...] = jnp.broadcast_to(l, l_s.shape)
        acc_s[...] = acc
        if mode == "fox":
            carry_s[...] = jnp.zeros(carry_s.shape, F32)

    ss = [_dot(qbd_b, kp[g][0].astype(BF16)) for g in range(n_grp)]
    if mode == "fox":
        kj = lax.broadcasted_iota(jnp.int32, (page, page), 0)
        ks = lax.broadcasted_iota(jnp.int32, (page, page), 1)
        later = (kj > ks).astype(F32)
        carry = carry_s[...]
        for g in range(n_grp):
            x = lfp[g][0]
            tail = jnp.dot(x, later, preferred_element_type=F32, precision=lax.Precision.HIGHEST) + carry
            carry = carry + jnp.sum(x, axis=1, keepdims=True)
            ss[g] = ss[g] + (rowb_l + jnp.concatenate([tail] * nq, axis=0))
        carry_s[...] = jnp.broadcast_to(carry, carry_s.shape)
    mx = ss[0]
    for s in ss[1:]:
        mx = jnp.maximum(mx, s)
    m_old = m_s[...]
    m_new = jnp.maximum(m_old, jnp.max(mx, axis=1, keepdims=True))
    alpha = jnp.exp(m_old - m_new)
    ps = [jnp.exp(s - m_new) for s in ss]
    lsum = ps[0]
    for p in ps[1:]:
        lsum = lsum + p
    l_s[...] = alpha * l_s[...] + jnp.sum(lsum, axis=1, keepdims=True)
    pv = None
    for g in range(n_grp):
        pb = ps[g].astype(BF16)
        if mode == "diff":
            t = jnp.concatenate([_dot(pb, vp[g][0, pl.ds(h, page, stride=rep), :].astype(BF16))
                                 for h in range(rep)], axis=1)
        else:
            t = lax.dot_general(pb, vp[g][0].astype(BF16), NT_DIMS, preferred_element_type=F32)
        pv = t if pv is None else pv + t
    acc_s[...] = jnp.concatenate([alpha] * rep, axis=1) * acc_s[...] + pv
    m_s[...] = m_new

    @pl.when(j == pl.num_programs(1) - 1)
    def _():
        o = acc_s[...] / jnp.concatenate([l_s[...]] * rep, axis=1)
        c8 = lax.broadcasted_iota(jnp.int32, (nchunk, w), 0)
        l8w = lax.broadcasted_iota(jnp.int32, (nchunk, w), 1)
        if mode == "diff":
            sign = jnp.where(c8 % 2 == 0, 1.0, -_lam(lq1, lk1, lq2, lk2))
            coef = jnp.where(l8w // (2 * HEAD_DIM) == c8 // 2, sign, 0.0)
        else:
            coef = jnp.where(l8w // HEAD_DIM == c8, 1.0, 0.0)
        o4 = jnp.sum(o.reshape(nq, nchunk, w) * coef[None], axis=1)
        if mode == "diff":
            hw = 2 * HEAD_DIM
            o4 = jnp.concatenate(
                [_rms(o4[:, h * hw:(h + 1) * hw], SUBLN_EPS) * gs_ref[...] for h in range(w // hw)],
                axis=1) * (1.0 - LAMBDA_INIT)
        o_ref[0] = o4.astype(BF16)


def _decode_page_map(b, j, pt, *, g, n_grp, n_pages, ndim):
    return (pt[b, n_pages - 1 - (j * n_grp + g)],) + (0,) * (ndim - 1)


def _decode_attention(page_table, q, k_new, v_new, extra, cache_kt, cache_v, cache_lft, mode):
    db, nq, w = q.shape
    n_pages = page_table.shape[1]
    n_grp = min(DECODE_PAGES, n_pages)
    seq = lambda a: pl.BlockSpec((1,) + a.shape[1:], lambda b, j, pt: (b,) + (0,) * (a.ndim - 1))
    small = lambda a: pl.BlockSpec(a.shape, lambda b, j, pt: (0,) * a.ndim)
    pages = lambda a: [pl.BlockSpec((1,) + a.shape[1:],
                                    functools.partial(_decode_page_map, g=g, n_grp=n_grp, n_pages=n_pages, ndim=a.ndim))
                       for g in range(n_grp)]
    in_specs = [seq(q), seq(k_new), seq(v_new)]
    if mode == "diff":
        in_specs += [small(a) for a in extra]
    else:
        in_specs += [seq(extra[0])]
    in_specs += pages(cache_kt) + pages(cache_v)
    args = [q, k_new, v_new, *extra] + [cache_kt] * n_grp + [cache_v] * n_grp
    nchunk = w // HEAD_DIM
    rows = nq * nchunk
    scratch = [pltpu.VMEM((rows, LANES), F32), pltpu.VMEM((rows, LANES), F32), pltpu.VMEM((rows, w), F32)]
    if mode == "fox":
        in_specs += pages(cache_lft)
        args += [cache_lft] * n_grp
        scratch.append(pltpu.VMEM((nchunk, LANES), F32))
    return pl.pallas_call(
        functools.partial(_decode_kernel, n_grp=n_grp, mode=mode),
        grid_spec=pltpu.PrefetchScalarGridSpec(
            num_scalar_prefetch=1,
            grid=(db, n_pages // n_grp),
            in_specs=in_specs,
            out_specs=pl.BlockSpec((1, nq, w), lambda b, j, pt: (b, 0, 0)),
            scratch_shapes=scratch),
        out_shape=jax.ShapeDtypeStruct((db, nq, w), BF16),
        compiler_params=pltpu.CompilerParams(dimension_semantics=("arbitrary", "arbitrary"),
                                             vmem_limit_bytes=VMEM_LIMIT),
        name="sample_attention_" + mode,
    )(page_table, *args)


def _merge_kernel(oa_ref, ob_ref, ga_ref, gb_ref, x_ref, g1_ref, sh2_ref, sc2_ref,
                  wa_ref, wb_ref, wo_ref, gpost_ref, gpre_ref, wr_hi_ref, wr_lo_ref, br_ref, cnt_in_ref,
                  *rest, n_alias):
    x1_ref, h2_ref, pos_ref, w_ref, tcnt_ref, cbefore_ref, cnt_ref, carry_s = rest[n_alias:]
    i = pl.program_id(0)

    @pl.when(i == 0)
    def _():
        carry_s[...] = cnt_in_ref[...]

    d = x_ref.shape[-1]
    y = ga_ref[...] * _dot(oa_ref[...], wa_ref[...]) + gb_ref[...] * _dot(ob_ref[...], wb_ref[...])
    y = _dot(y.astype(BF16), wo_ref[...])
    x1 = x_ref[...] + g1_ref[...].reshape(-1, d) * (_rms(y, NORM_EPS) * gpost_ref[...])
    x1_ref[...] = x1
    h = _rms(x1, NORM_EPS) * gpre_ref[...]
    h = h * (1.0 + sc2_ref[...].reshape(-1, d)) + sh2_ref[...].reshape(-1, d)
    h2_ref[...] = h

    h_hi, h_lo = _split_bf16(h)
    nt = lambda a, b: lax.dot_general(a, b, NT_DIMS, preferred_element_type=F32)
    logit = nt(wr_hi_ref[...], h_hi) + nt(wr_hi_ref[...], h_lo) + nt(wr_lo_ref[...], h_hi) + br_ref[...]
    ne, tm = logit.shape
    e_iota = lax.broadcasted_iota(jnp.int32, (ne, tm), 0)
    idxs, vals = [], []
    for _ in range(TOP_K):
        mx = jnp.max(logit, axis=0, keepdims=True)
        ik = jnp.min(jnp.where(logit == mx, e_iota, ne), axis=0, keepdims=True)
        idxs.append(ik)
        vals.append(mx)
        logit = jnp.where(e_iota == ik, -jnp.inf, logit)
    ex = [jnp.exp(v - vals[0]) for v in vals]
    den = ex[0]
    for e in ex[1:]:
        den = den + e
    onehot = jnp.zeros((ne, tm), F32)
    for ik in idxs:
        onehot = onehot + (e_iota == ik).astype(F32)
    onehot_b = onehot.astype(BF16)
    tr = lax.broadcasted_iota(jnp.int32, (tm, tm), 0)
    tc = lax.broadcasted_iota(jnp.int32, (tm, tm), 1)
    er = lax.broadcasted_iota(jnp.int32, (ne, ne), 0)
    ec = lax.broadcasted_iota(jnp.int32, (ne, ne), 1)
    tile_cnt = jnp.ceil(jnp.sum(onehot, axis=1, keepdims=True) * (1.0 / RUN_ALIGN)) * RUN_ALIGN
    lower = _dot((ec < er).astype(BF16), jnp.broadcast_to(tile_cnt, (ne, LANES)).astype(BF16))[:, 0:1]
    slot = _dot(onehot_b, (tr < tc).astype(BF16)) + lower
    for k in range(TOP_K):
        w_ref[k:k + 1, :] = ex[k] / den
        pos_ref[k:k + 1, :] = jnp.sum(jnp.where(e_iota == idxs[k], slot, 0.0), axis=0, keepdims=True).astype(jnp.int32)
    tcnt_ref[0] = tile_cnt
    cbefore_ref[0] = carry_s[...]
    carry_s[...] = carry_s[...] + tile_cnt
    cnt_ref[...] = carry_s[...]


def _merge_route(oa, ob, gates, x2d, g1, sh2, sc2, wa, wb, wo, gpost, gpre, wr_hi, wr_lo, br, cnt_in,
                 n_total, row_offset, prev, rows_per_seq, tm):
    n, d = x2d.shape
    off = row_offset // tm
    nt_total = n_total // tm
    if g1.ndim == 3:
        tps = rows_per_seq // tm
        mod_spec = pl.BlockSpec((1, 1, d), lambda i: (i // tps, 0, 0))
    else:
        mod_spec = pl.BlockSpec((tm, d), lambda i: (i, 0))
    const = lambda a: pl.BlockSpec(a.shape, lambda i: (0, 0), pipeline_mode=pl.Buffered(1))
    row = lambda w, c=0: pl.BlockSpec((tm, w), lambda i: (i, c))
    anyspec = pl.BlockSpec(memory_space=pl.ANY)
    n_alias = len(prev)
    per_tile = jax.ShapeDtypeStruct((nt_total, N_EXPERTS, 1), F32)
    per_tile_spec = pl.BlockSpec((1, N_EXPERTS, 1), lambda i: (i + off, 0, 0))
    out_shape = [jax.ShapeDtypeStruct((n_total, d), F32), jax.ShapeDtypeStruct((n_total, d), F32),
                 jax.ShapeDtypeStruct((TOP_K, n_total), jnp.int32), jax.ShapeDtypeStruct((TOP_K, n_total), F32),
                 per_tile, per_tile, jax.ShapeDtypeStruct((N_EXPERTS, 1), F32)]
    out_specs = [pl.BlockSpec((tm, d), lambda i: (i + off, 0)), pl.BlockSpec((tm, d), lambda i: (i + off, 0)),
                 pl.BlockSpec((TOP_K, tm), lambda i: (0, i + off)), pl.BlockSpec((TOP_K, tm), lambda i: (0, i + off)),
                 per_tile_spec, per_tile_spec, pl.BlockSpec((N_EXPERTS, 1), lambda i: (0, 0))]
    n_in = 17
    return pl.pallas_call(
        functools.partial(_merge_kernel, n_alias=n_alias),
        grid=(n // tm,),
        in_specs=[row(oa.shape[1]), row(ob.shape[1]), row(d, 0), row(d, 1), row(d), mod_spec, mod_spec, mod_spec,
                  const(wa), const(wb), const(wo), const(gpost), const(gpre), const(wr_hi), const(wr_lo), const(br),
                  const(cnt_in)] + [anyspec] * n_alias,
        out_specs=out_specs,
        out_shape=out_shape,
        input_output_aliases={n_in + a: a for a in range(n_alias)},
        scratch_shapes=[pltpu.VMEM((N_EXPERTS, 1), F32)],
        compiler_params=pltpu.CompilerParams(dimension_semantics=("arbitrary",), vmem_limit_bytes=VMEM_LIMIT),
        name="merge_and_route",
    )(oa, ob, gates, gates, x2d, g1, sh2, sc2, wa, wb, wo, gpost, gpre, wr_hi, wr_lo, br, cnt_in, *prev)


def _tile_rows(tm):
    return TOP_K * tm + RUN_ALIGN * N_EXPERTS


def _run_copies(cnt_ref, dst_ref, tile, tm, make_copy, wait=False):
    def body(e, src):
        count = cnt_ref[tile * N_EXPERTS + e]
        dst = dst_ref[tile * N_EXPERTS + e]
        size = tm
        while size >= RUN_ALIGN:
            above = (count // (2 * size)) * (2 * size)

            @pl.when((count & size) != 0)
            def _(above=above, size=size):
                copy = make_copy(pl.multiple_of(src + above, RUN_ALIGN), pl.multiple_of(dst + above, RUN_ALIGN), size)
                if wait:
                    copy.wait()
                else:
                    copy.start()

            size //= 2
        return src + count

    lax.fori_loop(0, N_EXPERTS, body, 0)


def _dispatch_kernel(cnt_ref, dst_ref, pos_ref, h_ref, xd_ref, xs_s, sem):
    i = pl.program_id(0)
    n_tiles = pl.num_programs(0)
    slot = i % 2
    tm = h_ref.shape[0]
    rows = xs_s.shape[1]

    def copies(tile, of_slot, wait):
        _run_copies(cnt_ref, dst_ref, tile, tm,
                    lambda src, dst, size: pltpu.make_async_copy(
                        xs_s.at[of_slot, pl.ds(src, size), :], xd_ref.at[pl.ds(dst, size), :], sem.at[of_slot]),
                    wait=wait)

    @pl.when(i >= 2)
    def _():
        copies(i - 2, slot, True)

    pos = pos_ref[...]
    p_iota = lax.broadcasted_iota(jnp.int32, (rows, tm), 0)
    sel = p_iota == pos[0:1]
    for k in range(1, TOP_K):
        sel = sel | (p_iota == pos[k:k + 1])
    xs_s[slot] = _dot(jnp.where(sel, 1.0, 0.0).astype(BF16), h_ref[...].astype(BF16))
    copies(i, slot, False)

    @pl.when(i == n_tiles - 1)
    def _():
        copies(i, slot, True)

        @pl.when(n_tiles > 1)
        def _():
            copies(i - 1, 1 - slot, True)


def _dispatch(tile_cnt, tile_dst, pos, h2, n_slots, tm):
    n, d = h2.shape
    return pl.pallas_call(
        _dispatch_kernel,
        grid_spec=pltpu.PrefetchScalarGridSpec(
            num_scalar_prefetch=2,
            grid=(n // tm,),
            in_specs=[pl.BlockSpec((TOP_K, tm), lambda i, c, s: (0, i)),
                      pl.BlockSpec((tm, d), lambda i, c, s: (i, 0))],
            out_specs=pl.BlockSpec(memory_space=pl.ANY),
            scratch_shapes=[pltpu.VMEM((2, _tile_rows(tm), d), F32), pltpu.SemaphoreType.DMA((2,))]),
        out_shape=jax.ShapeDtypeStruct((n_slots, d), F32),
        compiler_params=pltpu.CompilerParams(dimension_semantics=("arbitrary",), vmem_limit_bytes=VMEM_LIMIT),
        name="moe_dispatch",
    )(tile_cnt, tile_dst, pos, h2)


def _expert_kernel(be_ref, nv_ref, nu_ref, x_ref, w1_ref, b1_ref, w2_ref, b2_ref, y_ref):
    del be_ref
    i = pl.program_id(0)

    @pl.when(i < nu_ref[0])
    def _():
        x = x_ref[...]
        row = lax.broadcasted_iota(jnp.int32, (x.shape[0], 1), 0)
        x = jnp.where(row < nv_ref[i], x, 0.0).astype(BF16)
        hh = _dot(x, w1_ref[0]) + b1_ref[0]
        f = hh.shape[1] // 2
        x_glu = jnp.minimum(hh[:, :f], SWIGLU_LIMIT)
        x_lin = jnp.clip(hh[:, f:], -SWIGLU_LIMIT, SWIGLU_LIMIT)
        act = x_glu * jax.nn.sigmoid(SWIGLU_ALPHA * x_glu) * (x_lin + 1.0)
        y_ref[...] = _dot(act.astype(BF16), w2_ref[0]) + b2_ref[0]


def _experts(block_expert, n_valid, n_used, x_disp, w1, b1, w2, b2):
    n_slots, d = x_disp.shape
    tb = EXPERT_TILE
    ne, _, f2 = w1.shape
    blk = lambda i, be, nv, nu: (jnp.minimum(i, nu[0] - 1), 0)
    wsel = lambda i, be, nv, nu: (be[i], 0, 0)
    return pl.pallas_call(
        _expert_kernel,
        grid_spec=pltpu.PrefetchScalarGridSpec(
            num_scalar_prefetch=3,
            grid=(n_slots // tb,),
            in_specs=[pl.BlockSpec((tb, d), blk),
                      pl.BlockSpec((1, d, f2), wsel), pl.BlockSpec((1, 1, f2), wsel),
                      pl.BlockSpec((1, f2 // 2, d), wsel), pl.BlockSpec((1, 1, d), wsel)],
            out_specs=pl.BlockSpec((tb, d), blk)),
        out_shape=jax.ShapeDtypeStruct((n_slots, d), F32),
        compiler_params=pltpu.CompilerParams(dimension_semantics=("arbitrary",), vmem_limit_bytes=VMEM_LIMIT),
        name="moe_experts",
    )(block_expert, n_valid, n_used, x_disp, w1, b1.reshape(ne, 1, f2), w2, b2.reshape(ne, 1, d))


def _combine_kernel(cnt_ref, dst_ref, pos_ref, w_ref, x1_ref, g2_ref, gpost_ref, yd_ref, o_ref, ys_s, sem, *, tile_off):
    i = pl.program_id(0)
    n_local = pl.num_programs(0)
    slot = i % 2
    tm, d = x1_ref.shape
    rows = ys_s.shape[1]

    def gather(tile, to_slot, wait=False):
        _run_copies(cnt_ref, dst_ref, tile, tm,
                    lambda src, dst, size: pltpu.make_async_copy(
                        yd_ref.at[pl.ds(dst, size), :], ys_s.at[to_slot, pl.ds(src, size), :], sem.at[to_slot]),
                    wait=wait)

    @pl.when(i == 0)
    def _():
        ys_s[...] = jnp.zeros(ys_s.shape, F32)
        gather(tile_off, 0)

    @pl.when(i + 1 < n_local)
    def _():
        gather(tile_off + i + 1, 1 - slot)

    gather(tile_off + i, slot, wait=True)
    pos_t = pos_ref[...].astype(F32).T.astype(jnp.int32)
    w_t = w_ref[...].T
    p_iota = lax.broadcasted_iota(jnp.int32, (tm, rows), 1)
    sel = jnp.zeros((tm, rows), F32)
    for k in range(TOP_K):
        sel = sel + jnp.where(p_iota == pos_t[:, k:k + 1], w_t[:, k:k + 1], 0.0)
    s_hi, s_lo = _split_bf16(sel)
    yb = ys_s[slot].astype(BF16)
    y = _dot(s_hi, yb) + _dot(s_lo, yb)
    o_ref[...] = x1_ref[...] + g2_ref[...].reshape(-1, d) * (_rms(y, NORM_EPS) * gpost_ref[...])


def _combine(tile_cnt, tile_dst, pos, w, x1, g2, gpost, y_disp, n, row_offset, rows_per_seq, tm):
    d = x1.shape[1]
    off = row_offset // tm
    if g2.ndim == 3:
        tps = rows_per_seq // tm
        mod_spec = pl.BlockSpec((1, 1, d), lambda i, c, s: (i // tps, 0, 0))
    else:
        mod_spec = pl.BlockSpec((tm, d), lambda i, c, s: (i, 0))
    return pl.pallas_call(
        functools.partial(_combine_kernel, tile_off=off),
        grid_spec=pltpu.PrefetchScalarGridSpec(
            num_scalar_prefetch=2,
            grid=(n // tm,),
            in_specs=[pl.BlockSpec((TOP_K, tm), lambda i, c, s: (0, i + off)),
                      pl.BlockSpec((TOP_K, tm), lambda i, c, s: (0, i + off)),
                      pl.BlockSpec((tm, d), lambda i, c, s: (i + off, 0)),
                      mod_spec,
                      pl.BlockSpec((1, d), lambda i, c, s: (0, 0)),
                      pl.BlockSpec(memory_space=pl.ANY)],
            out_specs=pl.BlockSpec((tm, d), lambda i, c, s: (i, 0)),
            scratch_shapes=[pltpu.VMEM((2, _tile_rows(tm), d), F32), pltpu.SemaphoreType.DMA((2,))]),
        out_shape=jax.ShapeDtypeStruct((n, d), F32),
        compiler_params=pltpu.CompilerParams(dimension_semantics=("arbitrary",), vmem_limit_bytes=VMEM_LIMIT),
        name="moe_combine",
    )(tile_cnt, tile_dst, pos, w, x1, g2, gpost, y_disp)


def kernel(x_prompt, x_sample, c_prompt, c_sample, cache_a_k, cache_a_v, cache_b_k, cache_b_v, cache_b_logf, page_table, w_mod, b_mod, g_pre_mix, g_post_mix, g_pre_ffn, g_post_ffn, w_in, b_forget, b_gate, lambda_q1, lambda_k1, lambda_q2, lambda_k2, g_subln, w_br_a, w_br_b, w_out, w_router, b_router, w_mlp1, b_mlp1, w_mlp2, b_mlp2):
    bsz, seq, d = x_prompt.shape
    db, nq, _ = x_sample.shape
    n_pool, page = cache_a_k.shape[1], cache_a_k.shape[2]
    n_pages = page_table.shape[1]
    past_len = n_pages * page
    a_heads = cache_a_k.shape[3]
    b_heads = cache_b_k.shape[3]
    n_p, n_s = bsz * seq, db * nq
    n_tot = n_p + n_s
    w = QKV_WIDTH

    w_in0 = w_in[0]
    wqkv = w_in0[:, :6 * w].astype(BF16)
    wf_cols = w_in0[:, 6 * w:6 * w + b_heads]
    wf = jnp.pad(wf_cols, ((0, 0), (0, LANES - b_heads))).astype(BF16)
    wg = w_in0[:, 6 * w + b_heads:].astype(BF16)
    bf = jnp.pad(b_forget[0], (0, LANES - b_heads)).reshape(1, LANES)
    bg = b_gate[0].reshape(1, -1)
    wq = jnp.concatenate([wqkv[:, 0:w], wqkv[:, 3 * w:4 * w]], axis=1)
    wv = wqkv[:, 2 * w:3 * w]
    wt = jnp.concatenate([wqkv[:, w:2 * w], wqkv[:, 4 * w:5 * w], wqkv[:, 5 * w:6 * w]], axis=1).T
    wft = jnp.pad(wf_cols.T, ((0, 16 - b_heads), (0, 0))).astype(BF16)
    bft = b_forget[0].reshape(b_heads, 1)
    row = lambda a: a[0].reshape(1, -1)
    lam_args = (row(lambda_q1), row(lambda_k1), row(lambda_q2), row(lambda_k2), row(g_subln))
    wr_t = w_router[0].T
    wr_hi = wr_t.astype(BF16)
    wr_lo = (wr_t - wr_hi.astype(F32)).astype(BF16)
    br = b_router[0].reshape(N_EXPERTS, 1)

    mod = _adaln(jnp.concatenate([c_prompt, c_sample], axis=0), w_mod[0], b_mod[0])
    mod_p = [mod[i, :bsz].reshape(bsz, 1, d) for i in range(6)]
    mod_s = [jnp.repeat(mod[i, bsz:], nq, axis=0) for i in range(6)]

    pos_p = jnp.arange(seq, dtype=F32)
    ang_t = _rope_angles(pos_p).T
    tabs_s = tuple(jnp.tile(t, (db, 1)) for t in _rope_tables(past_len + jnp.arange(nq, dtype=F32)))
    xp2 = x_prompt.reshape(n_p, d)
    xs2 = x_sample.reshape(n_s, d)
    qa, kat, va, vab, qb, kbt, vbt, lft, gates = _project_prompt(
        xp2, mod_p[0], mod_p[1], row(g_pre_mix), _rope_tables(pos_p), (jnp.cos(ang_t), jnp.sin(ang_t)),
        wq, wv, wt, wft, wg, bft, bg, bsz, seq, a_heads, b_heads)
    qa_s, ka_s, va_s, qb_s, kb_s, vb_s, lf_s, gates_s = _project_sample(
        xs2, mod_s[0], mod_s[1], row(g_pre_mix), tabs_s, wqkv, wf, wg, bf, bg, b_heads)

    sh3 = lambda a: a.reshape(bsz, seq, w)
    oa = _prompt_attention(sh3(qa), kat, sh3(vab), lam_args, "diff")
    ob = _prompt_attention(sh3(qb), kbt, vbt, _cum_logf(lft), "fox")

    sd3 = lambda a: a.reshape(db, nq, -1)
    kt_a = jnp.transpose(cache_a_k[0], (0, 2, 3, 4, 1)).reshape(n_pool, w, page)
    kt_b = jnp.transpose(cache_b_k[0], (0, 2, 3, 1)).reshape(n_pool, w, page)
    vt_b = jnp.transpose(cache_b_v[0], (0, 2, 3, 1)).reshape(n_pool, w, page)
    lft_b = jnp.transpose(cache_b_logf[0], (0, 2, 1))
    oa_s = _decode_attention(page_table, sd3(qa_s), sd3(ka_s), sd3(va_s), lam_args,
                             kt_a, cache_a_v[0].reshape(n_pool, page * a_heads, 2 * HEAD_DIM), None, "diff")
    ob_s = _decode_attention(page_table, sd3(qb_s), sd3(kb_s), sd3(vb_s), (sd3(lf_s),),
                             kt_b, vt_b, lft_b, "fox")

    wa, wb, wo = w_br_a[0].astype(BF16), w_br_b[0].astype(BF16), w_out[0].astype(BF16)
    common = (wa, wb, wo, row(g_post_mix), row(g_pre_ffn), wr_hi, wr_lo, br)
    tm = math.gcd(TOKEN_TILE, n_s)
    outs_p = _merge_route(oa.reshape(n_p, w), ob.reshape(n_p, w), gates, xp2, mod_p[2], mod_p[3], mod_p[4], *common,
                          jnp.zeros((N_EXPERTS, 1), F32), n_tot, 0, (), seq, tm)
    x1, h2, pos, wts, tile_cnt, tile_before, counts = _merge_route(
        oa_s.reshape(n_s, w), ob_s.reshape(n_s, w), gates_s, xs2, mod_s[2], mod_s[3], mod_s[4], *common,
        outs_p[6], n_tot, n_p, tuple(outs_p[:6]), nq, tm)

    tb = EXPERT_TILE
    max_rows = n_tot * TOP_K + (RUN_ALIGN - 1) * N_EXPERTS * (n_tot // tm)
    n_blocks = (max_rows + N_EXPERTS * (tb - 1)) // tb
    cnt = counts[:, 0].astype(jnp.int32)
    padded = (cnt + tb - 1) // tb * tb
    pad_end = jnp.cumsum(padded)
    pad_start = pad_end - padded
    blk_start = jnp.arange(n_blocks, dtype=jnp.int32) * tb
    block_expert = jnp.minimum(jnp.sum((blk_start[:, None] >= pad_end[None, :]).astype(jnp.int32), axis=1),
                               N_EXPERTS - 1)
    n_valid = jnp.clip(cnt[block_expert] - (blk_start - pad_start[block_expert]), 0, tb).astype(jnp.int32)
    n_used = (pad_end[-1:] // tb).astype(jnp.int32)
    run_cnt = tile_cnt.reshape(-1).astype(jnp.int32)
    run_dst = (tile_before.reshape(-1, N_EXPERTS).astype(jnp.int32) + pad_start[None, :]).reshape(-1)

    x_disp = _dispatch(run_cnt, run_dst, pos, h2, n_blocks * tb, tm)
    y_disp = _experts(block_expert, n_valid, n_used, x_disp,
                      w_mlp1[0].astype(BF16), b_mlp1[0], w_mlp2[0].astype(BF16), b_mlp2[0])
    y_p = _combine(run_cnt, run_dst, pos, wts, x1, mod_p[5], row(g_post_ffn), y_disp, n_p, 0, seq, tm)
    y_s = _combine(run_cnt, run_dst, pos, wts, x1, mod_s[5], row(g_post_ffn), y_disp, n_s, n_p, nq, tm)

    new_ka = jnp.transpose(kat.reshape(1, bsz, a_heads, 2, HEAD_DIM, seq), (0, 1, 5, 2, 3, 4))
    new_kb = jnp.transpose(kbt.reshape(1, bsz, b_heads, HEAD_DIM, seq), (0, 1, 4, 2, 3))
    new_vb = jnp.transpose(vbt.reshape(1, bsz, b_heads, HEAD_DIM, seq), (0, 1, 4, 2, 3))
    new_lf = jnp.transpose(lft.reshape(1, bsz, b_heads, seq), (0, 1, 3, 2))
    return (y_p.reshape(bsz, seq, d), y_s.reshape(db, nq, d),
            new_ka, va.reshape(1, bsz, seq, a_heads, 2 * HEAD_DIM), new_kb, new_vb, new_lf,
            ka_s.reshape(1, db, nq, a_heads, 2, HEAD_DIM), va_s.reshape(1, db, nq, a_heads, 2 * HEAD_DIM),
            kb_s.reshape(1, db, nq, b_heads, HEAD_DIM), vb_s.reshape(1, db, nq, b_heads, HEAD_DIM),
            lf_s.reshape(1, db, nq, b_heads))
```

```python
import functools
import math

import jax
import jax.numpy as jnp
from jax import lax
from jax.experimental import pallas as pl
from jax.experimental.pallas import tpu as pltpu

F32 = jnp.float32
BF16 = jnp.bfloat16

HEAD_DIM = 64
ROT_DIM = HEAD_DIM // 4
ROPE_THETA = 500000.0
ATTN_SCALE = HEAD_DIM ** -0.5
N_EXPERTS = 32
TOP_K = 4
SWIGLU_LIMIT = 7.0
SWIGLU_ALPHA = 1.702
NORM_EPS = 1e-6
SUBLN_EPS = 1e-5
LAMBDA_INIT = 0.8 - 0.6

LANES = 128
QKV_WIDTH = 512
TOKEN_TILE = 256
ATTN_TILE = 512
KV_TILES = 2
EXPERT_TILE = 512
RUN_ALIGN = 8
DECODE_PAGES = 16
VMEM_LIMIT = 56 * 1024 * 1024

NT_DIMS = (((1,), (1,)), ((), ()))


def _dot(a, b):
    return jnp.dot(a, b, preferred_element_type=F32)


def _split_bf16(x):
    hi = x.astype(BF16)
    lo = (x - hi.astype(F32)).astype(BF16)
    return hi, lo


def _rms(x, eps):
    return x * lax.rsqrt(jnp.mean(x * x, axis=-1, keepdims=True) + eps)


def _lam(lq1, lk1, lq2, lk2):
    a = jnp.exp(jnp.sum(lq1[...] * lk1[...], axis=1, keepdims=True))
    b = jnp.exp(jnp.sum(lq2[...] * lk2[...], axis=1, keepdims=True))
    return a - b + LAMBDA_INIT


def _mod_kernel(c_ref, w_ref, b_ref, o_ref):
    c = c_ref[...]
    s_hi, s_lo = _split_bf16(c * jax.nn.sigmoid(c))
    w_hi, w_lo = _split_bf16(w_ref[...])
    o_ref[0] = _dot(s_hi, w_hi) + _dot(s_lo, w_hi) + _dot(s_hi, w_lo) + b_ref[...]


def _adaln(c, w_mod, b_mod):
    n, d = c.shape
    width = w_mod.shape[1]
    return pl.pallas_call(
        _mod_kernel,
        grid=(width // d,),
        in_specs=[pl.BlockSpec((n, d), lambda j: (0, 0)),
                  pl.BlockSpec((d, d), lambda j: (0, j)),
                  pl.BlockSpec((1, d), lambda j: (0, j))],
        out_specs=pl.BlockSpec((1, n, d), lambda j: (j, 0, 0)),
        out_shape=jax.ShapeDtypeStruct((width // d, n, d), F32),
        name="adaln_mod",
    )(c, w_mod, b_mod.reshape(1, width))


def _normed_input(x_ref, shift_ref, scale_ref, g_ref):
    x = x_ref[...]
    d = x.shape[-1]
    h = _rms(x, NORM_EPS) * g_ref[...]
    return (h * (1.0 + scale_ref[...].reshape(-1, d)) + shift_ref[...].reshape(-1, d)).astype(BF16)


def _rope_lanes(y, cos, sa, sb):
    parts = []
    for g in range(y.shape[1] // LANES):
        yg = y[:, g * LANES:(g + 1) * LANES]
        parts.append(yg * cos + pltpu.roll(yg, ROT_DIM // 2, 1) * sa
                     + pltpu.roll(yg, LANES - ROT_DIM // 2, 1) * sb)
    return jnp.concatenate(parts, axis=1)


def _log_sigmoid(z):
    return jnp.minimum(z, 0.0) - jnp.log1p(jnp.exp(-jnp.abs(z)))


def _proj_kernel(x_ref, shift_ref, scale_ref, g_ref, cos_ref, sa_ref, sb_ref,
                 wqkv_ref, wf_ref, wg_ref, bf_ref, bg_ref,
                 qa_ref, ka_ref, va_ref, qb_ref, kb_ref, vb_ref, lf_ref, gate_ref):
    hb = _normed_input(x_ref, shift_ref, scale_ref, g_ref)
    cos, sa, sb = cos_ref[...], sa_ref[...], sb_ref[...]

    def seg(i):
        return _dot(hb, wqkv_ref[:, i * QKV_WIDTH:(i + 1) * QKV_WIDTH])

    qa_ref[...] = (_rope_lanes(seg(0), cos, sa, sb) * ATTN_SCALE).astype(BF16)
    ka_ref[...] = _rope_lanes(seg(1), cos, sa, sb)
    va_ref[...] = seg(2)
    qb_ref[...] = (seg(3) * ATTN_SCALE).astype(BF16)
    kb_ref[...] = seg(4)
    vb_ref[...] = seg(5)
    z = (_dot(hb, wf_ref[...]) + bf_ref[...])[:, :lf_ref.shape[-1]]
    lf_ref[...] = _log_sigmoid(z)
    gate_ref[...] = jax.nn.sigmoid(_dot(hb, wg_ref[...]) + bg_ref[...]).astype(BF16)


def _project_sample(x2d, shift, scale, g_pre, tabs, wqkv, wf, wg, bf, bg, n_heads_b):
    n, d = x2d.shape
    tm = min(TOKEN_TILE, n)
    const = lambda a: pl.BlockSpec(a.shape, lambda i: (0, 0), pipeline_mode=pl.Buffered(1))
    row = lambda w: pl.BlockSpec((tm, w), lambda i: (i, 0))
    gw = wg.shape[1]
    return pl.pallas_call(
        _proj_kernel,
        grid=(n // tm,),
        in_specs=[row(d), row(d), row(d), const(g_pre), row(LANES), row(LANES), row(LANES),
                  const(wqkv), const(wf), const(wg), const(bf), const(bg)],
        out_specs=[row(QKV_WIDTH)] * 6 + [row(n_heads_b), row(gw)],
        out_shape=[jax.ShapeDtypeStruct((n, QKV_WIDTH), BF16),
                   jax.ShapeDtypeStruct((n, QKV_WIDTH), F32),
                   jax.ShapeDtypeStruct((n, QKV_WIDTH), F32),
                   jax.ShapeDtypeStruct((n, QKV_WIDTH), BF16),
                   jax.ShapeDtypeStruct((n, QKV_WIDTH), F32),
                   jax.ShapeDtypeStruct((n, QKV_WIDTH), F32),
                   jax.ShapeDtypeStruct((n, n_heads_b), F32),
                   jax.ShapeDtypeStruct((n, gw), BF16)],
        compiler_params=pltpu.CompilerParams(dimension_semantics=("arbitrary",), vmem_limit_bytes=VMEM_LIMIT),
        name="sample_projection",
    )(x2d, shift, scale, g_pre, *tabs, wqkv, wf, wg, bf, bg)


def _proj_prompt_kernel(x_ref, shift_ref, scale_ref, g_ref, cos_ref, sa_ref, sb_ref, cost_ref, sint_ref,
                        wq_ref, wv_ref, wt_ref, wft_ref, wg_ref, bft_ref, bg_ref,
                        qa_ref, kat_ref, va_ref, vab_ref, qb_ref, kbt_ref, vbt_ref, lft_ref, gate_ref):
    hb = _normed_input(x_ref, shift_ref, scale_ref, g_ref)
    hbt = hb.T
    w = QKV_WIDTH
    qa = _rope_lanes(_dot(hb, wq_ref[:, :w]), cos_ref[...], sa_ref[...], sb_ref[...])
    qa_ref[...] = (qa * ATTN_SCALE).astype(BF16)
    qb_ref[...] = (_dot(hb, wq_ref[:, w:]) * ATTN_SCALE).astype(BF16)
    v = _dot(hb, wv_ref[...])
    vab_ref[...] = v.astype(BF16)
    for h in range(va_ref.shape[1]):
        va_ref[:, h, :] = v[:, h * LANES:(h + 1) * LANES]

    kat = _dot(wt_ref[0:w, :], hbt)
    cos_t, sin_t = cost_ref[...], sint_ref[...]
    half = ROT_DIM // 2
    parts = []
    for c in range(w // HEAD_DIM):
        base = c * HEAD_DIM
        x1 = kat[base:base + half]
        x2 = kat[base + half:base + ROT_DIM]
        parts += [x1 * cos_t - x2 * sin_t, x2 * cos_t + x1 * sin_t, kat[base + ROT_DIM:base + HEAD_DIM]]
    kat_ref[0] = jnp.concatenate(parts, axis=0)
    kbt_ref[0] = _dot(wt_ref[w:2 * w, :], hbt)
    vbt_ref[0] = _dot(wt_ref[2 * w:3 * w, :], hbt)
    nh = lft_ref.shape[1]
    lft_ref[0] = _log_sigmoid(_dot(wft_ref[...], hbt)[:nh] + bft_ref[...])
    gate_ref[...] = jax.nn.sigmoid(_dot(hb, wg_ref[...]) + bg_ref[...]).astype(BF16)


def _project_prompt(x2d, shift, scale, g_pre, tabs, tabs_t, wq, wv, wt, wft, wg, bft, bg, bsz, seq, a_heads, b_heads):
    n, d = x2d.shape
    tm = min(TOKEN_TILE, seq)
    tps = seq // tm
    w = QKV_WIDTH
    const = lambda a: pl.BlockSpec(a.shape, lambda i: (0, 0), pipeline_mode=pl.Buffered(1))
    row = lambda width: pl.BlockSpec((tm, width), lambda i: (i, 0))
    mod_spec = pl.BlockSpec((1, 1, d), lambda i: (i // tps, 0, 0))
    tab_spec = pl.BlockSpec((tm, LANES), lambda i: (i % tps, 0))
    tabt_spec = pl.BlockSpec((ROT_DIM // 2, tm), lambda i: (0, i % tps))
    tr_spec = lambda rows: pl.BlockSpec((1, rows, tm), lambda i: (i // tps, 0, i % tps))
    gw = wg.shape[1]
    return pl.pallas_call(
        _proj_prompt_kernel,
        grid=(n // tm,),
        in_specs=[row(d), mod_spec, mod_spec, const(g_pre), tab_spec, tab_spec, tab_spec, tabt_spec, tabt_spec,
                  const(wq), const(wv), const(wt), const(wft), const(wg), const(bft), const(bg)],
        out_specs=[row(w), tr_spec(w), pl.BlockSpec((tm, a_heads, LANES), lambda i: (i, 0, 0)), row(w),
                   row(w), tr_spec(w), tr_spec(w), tr_spec(b_heads), row(gw)],
        out_shape=[jax.ShapeDtypeStruct((n, w), BF16),
                   jax.ShapeDtypeStruct((bsz, w, seq), F32),
                   jax.ShapeDtypeStruct((n, a_heads, LANES), F32),
                   jax.ShapeDtypeStruct((n, w), BF16),
                   jax.ShapeDtypeStruct((n, w), BF16),
                   jax.ShapeDtypeStruct((bsz, w, seq), F32),
                   jax.ShapeDtypeStruct((bsz, w, seq), F32),
                   jax.ShapeDtypeStruct((bsz, b_heads, seq), F32),
                   jax.ShapeDtypeStruct((n, gw), BF16)],
        compiler_params=pltpu.CompilerParams(dimension_semantics=("arbitrary",), vmem_limit_bytes=VMEM_LIMIT),
        name="prompt_projection",
    )(x2d, shift, scale, g_pre, *tabs, *tabs_t, wq, wv, wt, wft, wg, bft, bg)


def _rope_angles(pos):
    half = ROT_DIM // 2
    inv_freq = jnp.power(ROPE_THETA, -jnp.arange(half, dtype=F32) * 2.0 / ROT_DIM)
    return pos[:, None] * inv_freq[None, :]


def _rope_tables(pos):
    half = ROT_DIM // 2
    ang = _rope_angles(pos)
    cos, sin = jnp.cos(ang), jnp.sin(ang)
    s = pos.shape[0]
    ones = jnp.ones((s, HEAD_DIM - ROT_DIM), F32)
    zeros = jnp.zeros((s, HEAD_DIM - ROT_DIM), F32)
    zh = jnp.zeros((s, half), F32)
    cos_t = jnp.concatenate([cos, cos, ones], axis=1)
    sa_t = jnp.concatenate([zh, sin, zeros], axis=1)
    sb_t = jnp.concatenate([-sin, zh, zeros], axis=1)
    rep = LANES // HEAD_DIM
    return tuple(jnp.tile(t, (1, rep)) for t in (cos_t, sa_t, sb_t))


def _cum_kernel(lft_ref, c_ref, ct_ref, *, chunk):
    nh, s = lft_ref.shape[1], lft_ref.shape[2]
    r = lax.broadcasted_iota(jnp.int32, (chunk, chunk), 0)
    c = lax.broadcasted_iota(jnp.int32, (chunk, chunk), 1)
    upto = (r <= c).astype(F32)
    carry = jnp.zeros((nh, 1), F32)
    for i in range(s // chunk):
        x = lft_ref[0, :, i * chunk:(i + 1) * chunk]
        cum = jnp.dot(x, upto, preferred_element_type=F32, precision=lax.Precision.HIGHEST) + carry
        carry = cum[:, chunk - 1:chunk]
        cum_n = cum.T
        for p in range(nh // 2):
            ct_ref[0, p, :, i * chunk:(i + 1) * chunk] = cum[2 * p:2 * p + 2, :]
            c_ref[0, p, i * chunk:(i + 1) * chunk, :] = cum_n[:, 2 * p:2 * p + 2]


def _cum_logf(lft):
    b, nh, s = lft.shape
    chunk = min(256, s)
    return pl.pallas_call(
        functools.partial(_cum_kernel, chunk=chunk),
        grid=(b,),
        in_specs=[pl.BlockSpec((1, nh, s), lambda i: (i, 0, 0))],
        out_specs=[pl.BlockSpec((1, nh // 2, s, 2), lambda i: (i, 0, 0, 0)),
                   pl.BlockSpec((1, nh // 2, 2, s), lambda i: (i, 0, 0, 0))],
        out_shape=[jax.ShapeDtypeStruct((b, nh // 2, s, 2), F32),
                   jax.ShapeDtypeStruct((b, nh // 2, 2, s), F32)],
        name="forget_cumsum",
    )(lft)


def _pattn_kernel(*refs, tq, mode):
    if mode == "diff":
        q_ref, kt_ref, v_ref, lq1, lk1, lq2, lk2, gs_ref, o_ref, kt_s, v_s, m_s, acc_s = refs
    else:
        q_ref, kt_ref, vt_ref, c_ref, ct_ref, o_ref, kt_s, v_s, m_s, acc_s = refs
    qi = pl.program_id(2)
    tk = KV_TILES * tq

    @pl.when(qi == 0)
    def _():
        kt_s[...] = kt_ref[0].astype(BF16)
        if mode == "diff":
            v_s[:, :LANES] = v_ref[0]
            v_s[:, LANES:] = jnp.ones((v_s.shape[0], LANES), BF16)
        else:
            v = vt_ref[0].T
            first = lax.broadcasted_iota(jnp.int32, v.shape, 1) < HEAD_DIM
            v_s[0] = jnp.where(first, v, 1.0).astype(BF16)
            v_s[1] = jnp.where(first, 1.0, v).astype(BF16)

    q = q_ref[0]
    lane = lax.broadcasted_iota(jnp.int32, q.shape, 1)
    zero = jnp.zeros_like(q)
    qz = (jnp.where(lane < HEAD_DIM, q, zero), jnp.where(lane >= HEAD_DIM, q, zero))
    m_s[...] = jnp.full(m_s.shape, -jnp.inf, F32)
    acc_s[...] = jnp.zeros(acc_s.shape, F32)
    if mode == "fox":
        cb = c_ref[0, 0]
        rowb = tuple(jnp.broadcast_to(cb[:, hf:hf + 1], (tq, LANES)) for hf in range(2))
    row = lax.broadcasted_iota(jnp.int32, (tq, LANES), 0)
    col = lax.broadcasted_iota(jnp.int32, (tq, LANES), 1)
    n_acc = acc_s.shape[2] // LANES

    def tile(off, width, masked):
        kt = kt_s[:, pl.ds(off, width)]
        for hf in range(2):
            vt = v_s[pl.ds(off, width), :] if mode == "diff" else v_s[hf, pl.ds(off, width), :]
            s = _dot(qz[hf], kt)
            chunks = []
            for c in range(width // LANES):
                sc = s[:, c * LANES:(c + 1) * LANES]
                if mode == "fox":
                    sc = sc + (rowb[hf] - ct_ref[0, 0, hf:hf + 1, pl.ds(off + c * LANES, LANES)])
                if masked:
                    sc = jnp.where(col + c * LANES <= row, sc, -jnp.inf)
                chunks.append(sc)
            mx = chunks[0]
            for sc in chunks[1:]:
                mx = jnp.maximum(mx, sc)
            m_old = m_s[hf]
            m_new = jnp.maximum(m_old, jnp.max(mx, axis=1, keepdims=True))
            alpha = jnp.exp(m_old - m_new)
            p = jnp.concatenate([jnp.exp(sc - m_new) for sc in chunks], axis=1).astype(BF16)
            acc_s[hf] = jnp.concatenate([alpha] * n_acc, axis=1) * acc_s[hf] + _dot(p, vt)
            m_s[hf] = m_new

    def body(j, carry):
        tile(pl.multiple_of(j * tk, tk), tk, False)
        return carry

    lax.fori_loop(0, qi // KV_TILES, body, 0)
    if KV_TILES == 2:
        @pl.when(qi % 2 == 1)
        def _():
            tile(pl.multiple_of((qi - 1) * tq, tq), tq, False)

    tile(pl.multiple_of(qi * tq, tq), tq, True)

    if mode == "diff":
        a0, a1 = acc_s[0], acc_s[1]
        o = a0[:, :LANES] / a0[:, LANES:] - _lam(lq1, lk1, lq2, lk2) * (a1[:, :LANES] / a1[:, LANES:])
        o = _rms(o, SUBLN_EPS) * gs_ref[...] * (1.0 - LAMBDA_INIT)
    else:
        a0, a1 = acc_s[0], acc_s[1]
        o = jnp.where(lane < HEAD_DIM, a0 / pltpu.roll(a0, HEAD_DIM, 1), a1 / pltpu.roll(a1, HEAD_DIM, 1))
    o_ref[0] = o.astype(BF16)


def _prompt_attention(q, kt, v, extra, mode):
    b, s, w = q.shape
    tq = min(ATTN_TILE, s)
    nblk = w // LANES
    qspec = pl.BlockSpec((1, tq, LANES), lambda bi, h, qi: (bi, qi, h))
    ktspec = pl.BlockSpec((1, LANES, s), lambda bi, h, qi: (bi, h, 0))
    if mode == "diff":
        small = lambda a: pl.BlockSpec(a.shape, lambda bi, h, qi: (0, 0))
        in_specs = [qspec, ktspec, pl.BlockSpec((1, s, LANES), lambda bi, h, qi: (bi, 0, h))] + [small(a) for a in extra]
        scratch = [pltpu.VMEM((LANES, s), BF16), pltpu.VMEM((s, 2 * LANES), BF16),
                   pltpu.VMEM((2, tq, LANES), F32), pltpu.VMEM((2, tq, 2 * LANES), F32)]
    else:
        in_specs = [qspec, ktspec, ktspec,
                    pl.BlockSpec((1, 1, tq, 2), lambda bi, h, qi: (bi, h, qi, 0)),
                    pl.BlockSpec((1, 1, 2, s), lambda bi, h, qi: (bi, h, 0, 0))]
        scratch = [pltpu.VMEM((LANES, s), BF16), pltpu.VMEM((2, s, LANES), BF16),
                   pltpu.VMEM((2, tq, LANES), F32), pltpu.VMEM((2, tq, LANES), F32)]
    return pl.pallas_call(
        functools.partial(_pattn_kernel, tq=tq, mode=mode),
        grid=(b, nblk, s // tq),
        in_specs=in_specs,
        out_specs=qspec,
        out_shape=jax.ShapeDtypeStruct((b, s, w), BF16),
        scratch_shapes=scratch,
        compiler_params=pltpu.CompilerParams(dimension_semantics=("arbitrary", "arbitrary", "arbitrary"),
                                             vmem_limit_bytes=VMEM_LIMIT),
        name="prompt_attention_" + mode,
    )(q, kt, v, *extra)


def _decode_kernel(pt_ref, *refs, n_grp, mode):
    del pt_ref
    q_ref, kn_ref, vn_ref = refs[:3]
    pos = 3
    if mode == "diff":
        lq1, lk1, lq2, lk2, gs_ref = refs[pos:pos + 5]
        pos += 5
    else:
        lfn_ref = refs[pos]
        pos += 1
    kp = refs[pos:pos + n_grp]
    vp = refs[pos + n_grp:pos + 2 * n_grp]
    pos += 2 * n_grp
    if mode == "fox":
        lfp = refs[pos:pos + n_grp]
        pos += n_grp
    o_ref, m_s, l_s, acc_s = refs[pos:pos + 4]
    if mode == "fox":
        carry_s = refs[pos + 4]
    j = pl.program_id(1)
    nq, w = q_ref.shape[1], q_ref.shape[2]
    nchunk = w // HEAD_DIM
    rows = nq * nchunk
    page = kp[0].shape[2]
    rep = w // LANES

    def rep_rows(x):
        return jnp.concatenate([jnp.broadcast_to(x[i:i + 1], (nchunk, x.shape[1])) for i in range(nq)], axis=0)

    rid = lax.broadcasted_iota(jnp.int32, (rows, w), 0)
    lid = lax.broadcasted_iota(jnp.int32, (rows, w), 1)
    qbd = jnp.where(lid // HEAD_DIM == rid % nchunk, rep_rows(q_ref[0].astype(F32)), 0.0)
    qbd_b = qbd.astype(BF16)
    if mode == "fox":
        lfn = lfn_ref[0]
        cn = [lfn[0:1]]
        for i in range(1, nq):
            cn.append(cn[-1] + lfn[i:i + 1])
        r8 = lax.broadcasted_iota(jnp.int32, (rows, nchunk), 0)
        l8 = lax.broadcasted_iota(jnp.int32, (rows, nchunk), 1)
        sel = l8 == r8 % nchunk
        pick = lambda x: jnp.sum(jnp.where(sel, x, 0.0), axis=1, keepdims=True)
        rowb = pick(jnp.concatenate([jnp.broadcast_to(c, (nchunk, nchunk)) for c in cn], axis=0))
        rowb_l = jnp.broadcast_to(rowb, (rows, LANES))

    @pl.when(j == 0)
    def _():
        kn, vn = kn_ref[0], vn_ref[0]
        qq = lax.broadcasted_iota(jnp.int32, (rows, 1), 0) // nchunk
        ss = []
        for jn in range(nq):
            s = jnp.sum(qbd * kn[jn:jn + 1], axis=1, keepdims=True)
            if mode == "fox":
                s = s + (rowb - pick(jnp.broadcast_to(cn[jn], (rows, nchunk))))
            ss.append(jnp.where(jn <= qq, s, -jnp.inf))
        m = ss[0]
        for s in ss[1:]:
            m = jnp.maximum(m, s)
        ps = [jnp.exp(s - m) for s in ss]
        l = ps[0]
        acc = ps[0] * vn[0:1]
        for jn in range(1, nq):
            l = l + ps[jn]
            acc = acc + ps[jn] * vn[jn:jn + 1]
        m_s[...] = jnp.broadcast_to(m, m_s.shape)
        l_s[...] = jnp.broadcast_to(l, l_s.shape)
        acc_s[...] = acc
        if mode == "fox":
            carry_s[...] = jnp.zeros(carry_s.shape, F32)

    ss = [_dot(qbd_b, kp[g][0].astype(BF16)) for g in range(n_grp)]
    if mode == "fox":
        kj = lax.broadcasted_iota(jnp.int32, (page, page), 0)
        ks = lax.broadcasted_iota(jnp.int32, (page, page), 1)
        later = (kj > ks).astype(F32)
        carry = carry_s[...]
        for g in range(n_grp):
            x = lfp[g][0]
            tail = jnp.dot(x, later, preferred_element_type=F32, precision=lax.Precision.HIGHEST) + carry
            carry = carry + jnp.sum(x, axis=1, keepdims=True)
            ss[g] = ss[g] + (rowb_l + jnp.concatenate([tail] * nq, axis=0))
        carry_s[...] = jnp.broadcast_to(carry, carry_s.shape)
    mx = ss[0]
    for s in ss[1:]:
        mx = jnp.maximum(mx, s)
    m_old = m_s[...]
    m_new = jnp.maximum(m_old, jnp.max(mx, axis=1, keepdims=True))
    alpha = jnp.exp(m_old - m_new)
    ps = [jnp.exp(s - m_new) for s in ss]
    lsum = ps[0]
    for p in ps[1:]:
        lsum = lsum + p
    l_s[...] = alpha * l_s[...] + jnp.sum(lsum, axis=1, keepdims=True)
    pv = None
    for g in range(n_grp):
        pb = ps[g].astype(BF16)
        if mode == "diff":
            t = jnp.concatenate([_dot(pb, vp[g][0, pl.ds(h, page, stride=rep), :].astype(BF16))
                                 for h in range(rep)], axis=1)
        else:
            t = lax.dot_general(pb, vp[g][0].astype(BF16), NT_DIMS, preferred_element_type=F32)
        pv = t if pv is None else pv + t
    acc_s[...] = jnp.concatenate([alpha] * rep, axis=1) * acc_s[...] + pv
    m_s[...] = m_new

    @pl.when(j == pl.num_programs(1) - 1)
    def _():
        o = acc_s[...] / jnp.concatenate([l_s[...]] * rep, axis=1)
        c8 = lax.broadcasted_iota(jnp.int32, (nchunk, w), 0)
        l8w = lax.broadcasted_iota(jnp.int32, (nchunk, w), 1)
        if mode == "diff":
            sign = jnp.where(c8 % 2 == 0, 1.0, -_lam(lq1, lk1, lq2, lk2))
            coef = jnp.where(l8w // (2 * HEAD_DIM) == c8 // 2, sign, 0.0)
        else:
            coef = jnp.where(l8w // HEAD_DIM == c8, 1.0, 0.0)
        o4 = jnp.sum(o.reshape(nq, nchunk, w) * coef[None], axis=1)
        if mode == "diff":
            hw = 2 * HEAD_DIM
            o4 = jnp.concatenate(
                [_rms(o4[:, h * hw:(h + 1) * hw], SUBLN_EPS) * gs_ref[...] for h in range(w // hw)],
                axis=1) * (1.0 - LAMBDA_INIT)
        o_ref[0] = o4.astype(BF16)


def _decode_page_map(b, j, pt, *, g, n_grp, n_pages, ndim):
    return (pt[b, n_pages - 1 - (j * n_grp + g)],) + (0,) * (ndim - 1)


def _decode_attention(page_table, q, k_new, v_new, extra, cache_kt, cache_v, cache_lft, mode):
    db, nq, w = q.shape
    n_pages = page_table.shape[1]
    n_grp = min(DECODE_PAGES, n_pages)
    seq = lambda a: pl.BlockSpec((1,) + a.shape[1:], lambda b, j, pt: (b,) + (0,) * (a.ndim - 1))
    small = lambda a: pl.BlockSpec(a.shape, lambda b, j, pt: (0,) * a.ndim)
    pages = lambda a: [pl.BlockSpec((1,) + a.shape[1:],
                                    functools.partial(_decode_page_map, g=g, n_grp=n_grp, n_pages=n_pages, ndim=a.ndim))
                       for g in range(n_grp)]
    in_specs = [seq(q), seq(k_new), seq(v_new)]
    if mode == "diff":
        in_specs += [small(a) for a in extra]
    else:
        in_specs += [seq(extra[0])]
    in_specs += pages(cache_kt) + pages(cache_v)
    args = [q, k_new, v_new, *extra] + [cache_kt] * n_grp + [cache_v] * n_grp
    nchunk = w // HEAD_DIM
    rows = nq * nchunk
    scratch = [pltpu.VMEM((rows, LANES), F32), pltpu.VMEM((rows, LANES), F32), pltpu.VMEM((rows, w), F32)]
    if mode == "fox":
        in_specs += pages(cache_lft)
        args += [cache_lft] * n_grp
        scratch.append(pltpu.VMEM((nchunk, LANES), F32))
    return pl.pallas_call(
        functools.partial(_decode_kernel, n_grp=n_grp, mode=mode),
        grid_spec=pltpu.PrefetchScalarGridSpec(
            num_scalar_prefetch=1,
            grid=(db, n_pages // n_grp),
            in_specs=in_specs,
            out_specs=pl.BlockSpec((1, nq, w), lambda b, j, pt: (b, 0, 0)),
            scratch_shapes=scratch),
        out_shape=jax.ShapeDtypeStruct((db, nq, w), BF16),
        compiler_params=pltpu.CompilerParams(dimension_semantics=("arbitrary", "arbitrary"),
                                             vmem_limit_bytes=VMEM_LIMIT),
        name="sample_attention_" + mode,
    )(page_table, *args)


def _merge_kernel(oa_ref, ob_ref, ga_ref, gb_ref, x_ref, g1_ref, sh2_ref, sc2_ref,
                  wa_ref, wb_ref, wo_ref, gpost_ref, gpre_ref, wr_hi_ref, wr_lo_ref, br_ref, cnt_in_ref,
                  *rest, n_alias):
    x1_ref, h2_ref, pos_ref, w_ref, tcnt_ref, cbefore_ref, cnt_ref, carry_s = rest[n_alias:]
    i = pl.program_id(0)

    @pl.when(i == 0)
    def _():
        carry_s[...] = cnt_in_ref[...]

    d = x_ref.shape[-1]
    y = ga_ref[...] * _dot(oa_ref[...], wa_ref[...]) + gb_ref[...] * _dot(ob_ref[...], wb_ref[...])
    y = _dot(y.astype(BF16), wo_ref[...])
    x1 = x_ref[...] + g1_ref[...].reshape(-1, d) * (_rms(y, NORM_EPS) * gpost_ref[...])
    x1_ref[...] = x1
    h = _rms(x1, NORM_EPS) * gpre_ref[...]
    h = h * (1.0 + sc2_ref[...].reshape(-1, d)) + sh2_ref[...].reshape(-1, d)
    h2_ref[...] = h

    h_hi, h_lo = _split_bf16(h)
    nt = lambda a, b: lax.dot_general(a, b, NT_DIMS, preferred_element_type=F32)
    logit = nt(wr_hi_ref[...], h_hi) + nt(wr_hi_ref[...], h_lo) + nt(wr_lo_ref[...], h_hi) + br_ref[...]
    ne, tm = logit.shape
    e_iota = lax.broadcasted_iota(jnp.int32, (ne, tm), 0)
    idxs, vals = [], []
    for _ in range(TOP_K):
        mx = jnp.max(logit, axis=0, keepdims=True)
        ik = jnp.min(jnp.where(logit == mx, e_iota, ne), axis=0, keepdims=True)
        idxs.append(ik)
        vals.append(mx)
        logit = jnp.where(e_iota == ik, -jnp.inf, logit)
    ex = [jnp.exp(v - vals[0]) for v in vals]
    den = ex[0]
    for e in ex[1:]:
        den = den + e
    onehot = jnp.zeros((ne, tm), F32)
    for ik in idxs:
        onehot = onehot + (e_iota == ik).astype(F32)
    onehot_b = onehot.astype(BF16)
    tr = lax.broadcasted_iota(jnp.int32, (tm, tm), 0)
    tc = lax.broadcasted_iota(jnp.int32, (tm, tm), 1)
    er = lax.broadcasted_iota(jnp.int32, (ne, ne), 0)
    ec = lax.broadcasted_iota(jnp.int32, (ne, ne), 1)
    tile_cnt = jnp.ceil(jnp.sum(onehot, axis=1, keepdims=True) * (1.0 / RUN_ALIGN)) * RUN_ALIGN
    lower = _dot((ec < er).astype(BF16), jnp.broadcast_to(tile_cnt, (ne, LANES)).astype(BF16))[:, 0:1]
    slot = _dot(onehot_b, (tr < tc).astype(BF16)) + lower
    for k in range(TOP_K):
        w_ref[k:k + 1, :] = ex[k] / den
        pos_ref[k:k + 1, :] = jnp.sum(jnp.where(e_iota == idxs[k], slot, 0.0), axis=0, keepdims=True).astype(jnp.int32)
    tcnt_ref[0] = tile_cnt
    cbefore_ref[0] = carry_s[...]
    carry_s[...] = carry_s[...] + tile_cnt
    cnt_ref[...] = carry_s[...]


def _merge_route(oa, ob, gates, x2d, g1, sh2, sc2, wa, wb, wo, gpost, gpre, wr_hi, wr_lo, br, cnt_in,
                 n_total, row_offset, prev, rows_per_seq, tm):
    n, d = x2d.shape
    off = row_offset // tm
    nt_total = n_total // tm
    if g1.ndim == 3:
        tps = rows_per_seq // tm
        mod_spec = pl.BlockSpec((1, 1, d), lambda i: (i // tps, 0, 0))
    else:
        mod_spec = pl.BlockSpec((tm, d), lambda i: (i, 0))
    const = lambda a: pl.BlockSpec(a.shape, lambda i: (0, 0), pipeline_mode=pl.Buffered(1))
    row = lambda w, c=0: pl.BlockSpec((tm, w), lambda i: (i, c))
    anyspec = pl.BlockSpec(memory_space=pl.ANY)
    n_alias = len(prev)
    per_tile = jax.ShapeDtypeStruct((nt_total, N_EXPERTS, 1), F32)
    per_tile_spec = pl.BlockSpec((1, N_EXPERTS, 1), lambda i: (i + off, 0, 0))
    out_shape = [jax.ShapeDtypeStruct((n_total, d), F32), jax.ShapeDtypeStruct((n_total, d), F32),
                 jax.ShapeDtypeStruct((TOP_K, n_total), jnp.int32), jax.ShapeDtypeStruct((TOP_K, n_total), F32),
                 per_tile, per_tile, jax.ShapeDtypeStruct((N_EXPERTS, 1), F32)]
    out_specs = [pl.BlockSpec((tm, d), lambda i: (i + off, 0)), pl.BlockSpec((tm, d), lambda i: (i + off, 0)),
                 pl.BlockSpec((TOP_K, tm), lambda i: (0, i + off)), pl.BlockSpec((TOP_K, tm), lambda i: (0, i + off)),
                 per_tile_spec, per_tile_spec, pl.BlockSpec((N_EXPERTS, 1), lambda i: (0, 0))]
    n_in = 17
    return pl.pallas_call(
        functools.partial(_merge_kernel, n_alias=n_alias),
        grid=(n // tm,),
        in_specs=[row(oa.shape[1]), row(ob.shape[1]), row(d, 0), row(d, 1), row(d), mod_spec, mod_spec, mod_spec,
                  const(wa), const(wb), const(wo), const(gpost), const(gpre), const(wr_hi), const(wr_lo), const(br),
                  const(cnt_in)] + [anyspec] * n_alias,
        out_specs=out_specs,
        out_shape=out_shape,
        input_output_aliases={n_in + a: a for a in range(n_alias)},
        scratch_shapes=[pltpu.VMEM((N_EXPERTS, 1), F32)],
        compiler_params=pltpu.CompilerParams(dimension_semantics=("arbitrary",), vmem_limit_bytes=VMEM_LIMIT),
        name="merge_and_route",
    )(oa, ob, gates, gates, x2d, g1, sh2, sc2, wa, wb, wo, gpost, gpre, wr_hi, wr_lo, br, cnt_in, *prev)


def _tile_rows(tm):
    return TOP_K * tm + RUN_ALIGN * N_EXPERTS


def _run_copies(cnt_ref, dst_ref, tile, tm, make_copy, wait=False):
    def body(e, src):
        count = cnt_ref[tile * N_EXPERTS + e]
        dst = dst_ref[tile * N_EXPERTS + e]
        size = tm
        while size >= RUN_ALIGN:
            above = (count // (2 * size)) * (2 * size)

            @pl.when((count & size) != 0)
            def _(above=above, size=size):
                copy = make_copy(pl.multiple_of(src + above, RUN_ALIGN), pl.multiple_of(dst + above, RUN_ALIGN), size)
                if wait:
                    copy.wait()
                else:
                    copy.start()

            size //= 2
        return src + count

    lax.fori_loop(0, N_EXPERTS, body, 0)


def _dispatch_kernel(cnt_ref, dst_ref, pos_ref, h_ref, xd_ref, xs_s, sem):
    i = pl.program_id(0)
    n_tiles = pl.num_programs(0)
    slot = i % 2
    tm = h_ref.shape[0]
    rows = xs_s.shape[1]

    def copies(tile, of_slot, wait):
        _run_copies(cnt_ref, dst_ref, tile, tm,
                    lambda src, dst, size: pltpu.make_async_copy(
                        xs_s.at[of_slot, pl.ds(src, size), :], xd_ref.at[pl.ds(dst, size), :], sem.at[of_slot]),
                    wait=wait)

    @pl.when(i >= 2)
    def _():
        copies(i - 2, slot, True)

    pos = pos_ref[...]
    p_iota = lax.broadcasted_iota(jnp.int32, (rows, tm), 0)
    sel = p_iota == pos[0:1]
    for k in range(1, TOP_K):
        sel = sel | (p_iota == pos[k:k + 1])
    xs_s[slot] = _dot(jnp.where(sel, 1.0, 0.0).astype(BF16), h_ref[...].astype(BF16))
    copies(i, slot, False)

    @pl.when(i == n_tiles - 1)
    def _():
        copies(i, slot, True)

        @pl.when(n_tiles > 1)
        def _():
            copies(i - 1, 1 - slot, True)


def _dispatch(tile_cnt, tile_dst, pos, h2, n_slots, tm):
    n, d = h2.shape
    return pl.pallas_call(
        _dispatch_kernel,
        grid_spec=pltpu.PrefetchScalarGridSpec(
            num_scalar_prefetch=2,
            grid=(n // tm,),
            in_specs=[pl.BlockSpec((TOP_K, tm), lambda i, c, s: (0, i)),
                      pl.BlockSpec((tm, d), lambda i, c, s: (i, 0))],
            out_specs=pl.BlockSpec(memory_space=pl.ANY),
            scratch_shapes=[pltpu.VMEM((2, _tile_rows(tm), d), F32), pltpu.SemaphoreType.DMA((2,))]),
        out_shape=jax.ShapeDtypeStruct((n_slots, d), F32),
        compiler_params=pltpu.CompilerParams(dimension_semantics=("arbitrary",), vmem_limit_bytes=VMEM_LIMIT),
        name="moe_dispatch",
    )(tile_cnt, tile_dst, pos, h2)


def _expert_kernel(be_ref, nv_ref, nu_ref, x_ref, w1_ref, b1_ref, w2_ref, b2_ref, y_ref, w1_s, w2_s):
    i = pl.program_id(0)

    @pl.when(i < nu_ref[0])
    def _():
        @pl.when((i == 0) | (be_ref[i] != be_ref[jnp.maximum(i - 1, 0)]))
        def _():
            w1_s[...] = w1_ref[0].astype(BF16)
            w2_s[...] = w2_ref[0].astype(BF16)

        x = x_ref[...]
        row = lax.broadcasted_iota(jnp.int32, (x.shape[0], 1), 0)
        x = jnp.where(row < nv_ref[i], x, 0.0).astype(BF16)
        hh = _dot(x, w1_s[...]) + b1_ref[0]
        f = hh.shape[1] // 2
        x_glu = jnp.minimum(hh[:, :f], SWIGLU_LIMIT)
        x_lin = jnp.clip(hh[:, f:], -SWIGLU_LIMIT, SWIGLU_LIMIT)
        act = x_glu * jax.nn.sigmoid(SWIGLU_ALPHA * x_glu) * (x_lin + 1.0)
        y_ref[...] = _dot(act.astype(BF16), w2_s[...]) + b2_ref[0]


def _experts(block_expert, n_valid, n_used, x_disp, w1, b1, w2, b2):
    n_slots, d = x_disp.shape
    tb = EXPERT_TILE
    ne, _, f2 = w1.shape
    blk = lambda i, be, nv, nu: (jnp.minimum(i, nu[0] - 1), 0)
    wsel = lambda i, be, nv, nu: (be[i], 0, 0)
    return pl.pallas_call(
        _expert_kernel,
        grid_spec=pltpu.PrefetchScalarGridSpec(
            num_scalar_prefetch=3,
            grid=(n_slots // tb,),
            in_specs=[pl.BlockSpec((tb, d), blk),
                      pl.BlockSpec((1, d, f2), wsel), pl.BlockSpec((1, 1, f2), wsel),
                      pl.BlockSpec((1, f2 // 2, d), wsel), pl.BlockSpec((1, 1, d), wsel)],
            out_specs=pl.BlockSpec((tb, d), blk),
            scratch_shapes=[pltpu.VMEM((d, f2), BF16), pltpu.VMEM((f2 // 2, d), BF16)]),
        out_shape=jax.ShapeDtypeStruct((n_slots, d), F32),
        compiler_params=pltpu.CompilerParams(dimension_semantics=("arbitrary",), vmem_limit_bytes=VMEM_LIMIT),
        name="moe_experts",
    )(block_expert, n_valid, n_used, x_disp, w1, b1.reshape(ne, 1, f2), w2, b2.reshape(ne, 1, d))


def _combine_kernel(cnt_ref, dst_ref, pos_ref, w_ref, x1_ref, g2_ref, gpost_ref, yd_ref, o_ref, ys_s, sem, *, tile_off):
    i = pl.program_id(0)
    n_local = pl.num_programs(0)
    slot = i % 2
    tm, d = x1_ref.shape
    rows = ys_s.shape[1]

    def gather(tile, to_slot, wait=False):
        _run_copies(cnt_ref, dst_ref, tile, tm,
                    lambda src, dst, size: pltpu.make_async_copy(
                        yd_ref.at[pl.ds(dst, size), :], ys_s.at[to_slot, pl.ds(src, size), :], sem.at[to_slot]),
                    wait=wait)

    @pl.when(i == 0)
    def _():
        ys_s[...] = jnp.zeros(ys_s.shape, F32)
        gather(tile_off, 0)

    @pl.when(i + 1 < n_local)
    def _():
        gather(tile_off + i + 1, 1 - slot)

    gather(tile_off + i, slot, wait=True)
    pos_t = pos_ref[...].astype(F32).T.astype(jnp.int32)
    w_t = w_ref[...].T
    p_iota = lax.broadcasted_iota(jnp.int32, (tm, rows), 1)
    sel = jnp.zeros((tm, rows), F32)
    for k in range(TOP_K):
        sel = sel + jnp.where(p_iota == pos_t[:, k:k + 1], w_t[:, k:k + 1], 0.0)
    y = _dot(sel.astype(BF16), ys_s[slot].astype(BF16))
    o_ref[...] = x1_ref[...] + g2_ref[...].reshape(-1, d) * (_rms(y, NORM_EPS) * gpost_ref[...])


def _combine(tile_cnt, tile_dst, pos, w, x1, g2, gpost, y_disp, n, row_offset, rows_per_seq, tm):
    d = x1.shape[1]
    off = row_offset // tm
    if g2.ndim == 3:
        tps = rows_per_seq // tm
        mod_spec = pl.BlockSpec((1, 1, d), lambda i, c, s: (i // tps, 0, 0))
    else:
        mod_spec = pl.BlockSpec((tm, d), lambda i, c, s: (i, 0))
    return pl.pallas_call(
        functools.partial(_combine_kernel, tile_off=off),
        grid_spec=pltpu.PrefetchScalarGridSpec(
            num_scalar_prefetch=2,
            grid=(n // tm,),
            in_specs=[pl.BlockSpec((TOP_K, tm), lambda i, c, s: (0, i + off)),
                      pl.BlockSpec((TOP_K, tm), lambda i, c, s: (0, i + off)),
                      pl.BlockSpec((tm, d), lambda i, c, s: (i + off, 0)),
                      mod_spec,
                      pl.BlockSpec((1, d), lambda i, c, s: (0, 0)),
                      pl.BlockSpec(memory_space=pl.ANY)],
            out_specs=pl.BlockSpec((tm, d), lambda i, c, s: (i, 0)),
            scratch_shapes=[pltpu.VMEM((2, _tile_rows(tm), d), F32), pltpu.SemaphoreType.DMA((2,))]),
        out_shape=jax.ShapeDtypeStruct((n, d), F32),
        compiler_params=pltpu.CompilerParams(dimension_semantics=("arbitrary",), vmem_limit_bytes=VMEM_LIMIT),
        name="moe_combine",
    )(tile_cnt, tile_dst, pos, w, x1, g2, gpost, y_disp)


def kernel(x_prompt, x_sample, c_prompt, c_sample, cache_a_k, cache_a_v, cache_b_k, cache_b_v, cache_b_logf, page_table, w_mod, b_mod, g_pre_mix, g_post_mix, g_pre_ffn, g_post_ffn, w_in, b_forget, b_gate, lambda_q1, lambda_k1, lambda_q2, lambda_k2, g_subln, w_br_a, w_br_b, w_out, w_router, b_router, w_mlp1, b_mlp1, w_mlp2, b_mlp2):
    bsz, seq, d = x_prompt.shape
    db, nq, _ = x_sample.shape
    n_pool, page = cache_a_k.shape[1], cache_a_k.shape[2]
    n_pages = page_table.shape[1]
    past_len = n_pages * page
    a_heads = cache_a_k.shape[3]
    b_heads = cache_b_k.shape[3]
    n_p, n_s = bsz * seq, db * nq
    n_tot = n_p + n_s
    w = QKV_WIDTH

    w_in0 = w_in[0]
    wqkv = w_in0[:, :6 * w].astype(BF16)
    wf_cols = w_in0[:, 6 * w:6 * w + b_heads]
    wf = jnp.pad(wf_cols, ((0, 0), (0, LANES - b_heads))).astype(BF16)
    wg = w_in0[:, 6 * w + b_heads:].astype(BF16)
    bf = jnp.pad(b_forget[0], (0, LANES - b_heads)).reshape(1, LANES)
    bg = b_gate[0].reshape(1, -1)
    wq = jnp.concatenate([wqkv[:, 0:w], wqkv[:, 3 * w:4 * w]], axis=1)
    wv = wqkv[:, 2 * w:3 * w]
    wt = jnp.concatenate([wqkv[:, w:2 * w], wqkv[:, 4 * w:5 * w], wqkv[:, 5 * w:6 * w]], axis=1).T
    wft = jnp.pad(wf_cols.T, ((0, 16 - b_heads), (0, 0))).astype(BF16)
    bft = b_forget[0].reshape(b_heads, 1)
    row = lambda a: a[0].reshape(1, -1)
    lam_args = (row(lambda_q1), row(lambda_k1), row(lambda_q2), row(lambda_k2), row(g_subln))
    wr_t = w_router[0].T
    wr_hi = wr_t.astype(BF16)
    wr_lo = (wr_t - wr_hi.astype(F32)).astype(BF16)
    br = b_router[0].reshape(N_EXPERTS, 1)

    mod = _adaln(jnp.concatenate([c_prompt, c_sample], axis=0), w_mod[0], b_mod[0])
    mod_p = [mod[i, :bsz].reshape(bsz, 1, d) for i in range(6)]
    mod_s = [jnp.repeat(mod[i, bsz:], nq, axis=0) for i in range(6)]

    pos_p = jnp.arange(seq, dtype=F32)
    ang_t = _rope_angles(pos_p).T
    tabs_s = tuple(jnp.tile(t, (db, 1)) for t in _rope_tables(past_len + jnp.arange(nq, dtype=F32)))
    xp2 = x_prompt.reshape(n_p, d)
    xs2 = x_sample.reshape(n_s, d)
    qa, kat, va, vab, qb, kbt, vbt, lft, gates = _project_prompt(
        xp2, mod_p[0], mod_p[1], row(g_pre_mix), _rope_tables(pos_p), (jnp.cos(ang_t), jnp.sin(ang_t)),
        wq, wv, wt, wft, wg, bft, bg, bsz, seq, a_heads, b_heads)
    qa_s, ka_s, va_s, qb_s, kb_s, vb_s, lf_s, gates_s = _project_sample(
        xs2, mod_s[0], mod_s[1], row(g_pre_mix), tabs_s, wqkv, wf, wg, bf, bg, b_heads)

    sh3 = lambda a: a.reshape(bsz, seq, w)
    oa = _prompt_attention(sh3(qa), kat, sh3(vab), lam_args, "diff")
    ob = _prompt_attention(sh3(qb), kbt, vbt, _cum_logf(lft), "fox")

    sd3 = lambda a: a.reshape(db, nq, -1)
    kt_a = jnp.transpose(cache_a_k[0], (0, 2, 3, 4, 1)).reshape(n_pool, w, page)
    kt_b = jnp.transpose(cache_b_k[0], (0, 2, 3, 1)).reshape(n_pool, w, page)
    vt_b = jnp.transpose(cache_b_v[0], (0, 2, 3, 1)).reshape(n_pool, w, page)
    lft_b = jnp.transpose(cache_b_logf[0], (0, 2, 1))
    oa_s = _decode_attention(page_table, sd3(qa_s), sd3(ka_s), sd3(va_s), lam_args,
                             kt_a, cache_a_v[0].reshape(n_pool, page * a_heads, 2 * HEAD_DIM), None, "diff")
    ob_s = _decode_attention(page_table, sd3(qb_s), sd3(kb_s), sd3(vb_s), (sd3(lf_s),),
                             kt_b, vt_b, lft_b, "fox")

    wa, wb, wo = w_br_a[0].astype(BF16), w_br_b[0].astype(BF16), w_out[0].astype(BF16)
    common = (wa, wb, wo, row(g_post_mix), row(g_pre_ffn), wr_hi, wr_lo, br)
    tm = math.gcd(TOKEN_TILE, n_s)
    outs_p = _merge_route(oa.reshape(n_p, w), ob.reshape(n_p, w), gates, xp2, mod_p[2], mod_p[3], mod_p[4], *common,
                          jnp.zeros((N_EXPERTS, 1), F32), n_tot, 0, (), seq, tm)
    x1, h2, pos, wts, tile_cnt, tile_before, counts = _merge_route(
        oa_s.reshape(n_s, w), ob_s.reshape(n_s, w), gates_s, xs2, mod_s[2], mod_s[3], mod_s[4], *common,
        outs_p[6], n_tot, n_p, tuple(outs_p[:6]), nq, tm)

    tb = EXPERT_TILE
    max_rows = n_tot * TOP_K + (RUN_ALIGN - 1) * N_EXPERTS * (n_tot // tm)
    n_blocks = (max_rows + N_EXPERTS * (tb - 1)) // tb
    cnt = counts[:, 0].astype(jnp.int32)
    padded = (cnt + tb - 1) // tb * tb
    pad_end = jnp.cumsum(padded)
    pad_start = pad_end - padded
    blk_start = jnp.arange(n_blocks, dtype=jnp.int32) * tb
    block_expert = jnp.minimum(jnp.sum((blk_start[:, None] >= pad_end[None, :]).astype(jnp.int32), axis=1),
                               N_EXPERTS - 1)
    n_valid = jnp.clip(cnt[block_expert] - (blk_start - pad_start[block_expert]), 0, tb).astype(jnp.int32)
    n_used = (pad_end[-1:] // tb).astype(jnp.int32)
    run_cnt = tile_cnt.reshape(-1).astype(jnp.int32)
    run_dst = (tile_before.reshape(-1, N_EXPERTS).astype(jnp.int32) + pad_start[None, :]).reshape(-1)

    x_disp = _dispatch(run_cnt, run_dst, pos, h2, n_blocks * tb, tm)
    y_disp = _experts(block_expert, n_valid, n_used, x_disp,
                      w_mlp1[0], b_mlp1[0], w_mlp2[0], b_mlp2[0])
    y_p = _combine(run_cnt, run_dst, pos, wts, x1, mod_p[5], row(g_post_ffn), y_disp, n_p, 0, seq, tm)
    y_s = _combine(run_cnt, run_dst, pos, wts, x1, mod_s[5], row(g_post_ffn), y_disp, n_s, n_p, nq, tm)

    new_ka = jnp.transpose(kat.reshape(1, bsz, a_heads, 2, HEAD_DIM, seq), (0, 1, 5, 2, 3, 4))
    new_kb = jnp.transpose(kbt.reshape(1, bsz, b_heads, HEAD_DIM, seq), (0, 1, 4, 2, 3))
    new_vb = jnp.transpose(vbt.reshape(1, bsz, b_heads, HEAD_DIM, seq), (0, 1, 4, 2, 3))
    new_lf = jnp.transpose(lft.reshape(1, bsz, b_heads, seq), (0, 1, 3, 2))
    return (y_p.reshape(bsz, seq, d), y_s.reshape(db, nq, d),
            new_ka, va.reshape(1, bsz, seq, a_heads, 2 * HEAD_DIM), new_kb, new_vb, new_lf,
            ka_s.reshape(1, db, nq, a_heads, 2, HEAD_DIM), va_s.reshape(1, db, nq, a_heads, 2 * HEAD_DIM),
            kb_s.reshape(1, db, nq, b_heads, HEAD_DIM), vb_s.reshape(1, db, nq, b_heads, HEAD_DIM),
            lf_s.reshape(1, db, nq, b_heads))
```

```python
import functools
import math

import jax
import jax.numpy as jnp
from jax import lax
from jax.experimental import pallas as pl
from jax.experimental.pallas import tpu as pltpu

F32 = jnp.float32
BF16 = jnp.bfloat16

HEAD_DIM = 64
ROT_DIM = HEAD_DIM // 4
ROPE_THETA = 500000.0
ATTN_SCALE = HEAD_DIM ** -0.5
N_EXPERTS = 32
TOP_K = 4
SWIGLU_LIMIT = 7.0
SWIGLU_ALPHA = 1.702
NORM_EPS = 1e-6
SUBLN_EPS = 1e-5
LAMBDA_INIT = 0.8 - 0.6

LANES = 128
QKV_WIDTH = 512
TOKEN_TILE = 256
ATTN_TILE = 512
KV_TILES = 2
EXPERT_TILE = 512
RUN_ALIGN = 8
DECODE_PAGES = 16
VMEM_LIMIT = 56 * 1024 * 1024

NT_DIMS = (((1,), (1,)), ((), ()))


def _dot(a, b):
    return jnp.dot(a, b, preferred_element_type=F32)


def _split_bf16(x):
    hi = x.astype(BF16)
    lo = (x - hi.astype(F32)).astype(BF16)
    return hi, lo


def _rms(x, eps):
    return x * lax.rsqrt(jnp.mean(x * x, axis=-1, keepdims=True) + eps)


def _lam(lq1, lk1, lq2, lk2):
    a = jnp.exp(jnp.sum(lq1[...] * lk1[...], axis=1, keepdims=True))
    b = jnp.exp(jnp.sum(lq2[...] * lk2[...], axis=1, keepdims=True))
    return a - b + LAMBDA_INIT


def _mod_kernel(c_ref, w_ref, b_ref, o_ref):
    c = c_ref[...]
    s_hi, s_lo = _split_bf16(c * jax.nn.sigmoid(c))
    w_hi, w_lo = _split_bf16(w_ref[...])
    o_ref[0] = _dot(s_hi, w_hi) + _dot(s_lo, w_hi) + _dot(s_hi, w_lo) + b_ref[...]


def _adaln(c, w_mod, b_mod):
    n, d = c.shape
    width = w_mod.shape[1]
    return pl.pallas_call(
        _mod_kernel,
        grid=(width // d,),
        in_specs=[pl.BlockSpec((n, d), lambda j: (0, 0)),
                  pl.BlockSpec((d, d), lambda j: (0, j)),
                  pl.BlockSpec((1, d), lambda j: (0, j))],
        out_specs=pl.BlockSpec((1, n, d), lambda j: (j, 0, 0)),
        out_shape=jax.ShapeDtypeStruct((width // d, n, d), F32),
        name="adaln_mod",
    )(c, w_mod, b_mod.reshape(1, width))


def _normed_input(x_ref, shift_ref, scale_ref, g_ref):
    x = x_ref[...]
    d = x.shape[-1]
    h = _rms(x, NORM_EPS) * g_ref[...]
    return (h * (1.0 + scale_ref[...].reshape(-1, d)) + shift_ref[...].reshape(-1, d)).astype(BF16)


def _rope_lanes(y, cos, sa, sb):
    parts = []
    for g in range(y.shape[1] // LANES):
        yg = y[:, g * LANES:(g + 1) * LANES]
        parts.append(yg * cos + pltpu.roll(yg, ROT_DIM // 2, 1) * sa
                     + pltpu.roll(yg, LANES - ROT_DIM // 2, 1) * sb)
    return jnp.concatenate(parts, axis=1)


def _log_sigmoid(z):
    return jnp.minimum(z, 0.0) - jnp.log1p(jnp.exp(-jnp.abs(z)))


def _proj_kernel(x_ref, shift_ref, scale_ref, g_ref, cos_ref, sa_ref, sb_ref,
                 wqkv_ref, wf_ref, wg_ref, bf_ref, bg_ref,
                 qa_ref, ka_ref, va_ref, qb_ref, kb_ref, vb_ref, lf_ref, gate_ref):
    hb = _normed_input(x_ref, shift_ref, scale_ref, g_ref)
    cos, sa, sb = cos_ref[...], sa_ref[...], sb_ref[...]

    def seg(i):
        return _dot(hb, wqkv_ref[:, i * QKV_WIDTH:(i + 1) * QKV_WIDTH])

    qa_ref[...] = (_rope_lanes(seg(0), cos, sa, sb) * ATTN_SCALE).astype(BF16)
    ka_ref[...] = _rope_lanes(seg(1), cos, sa, sb)
    va_ref[...] = seg(2)
    qb_ref[...] = (seg(3) * ATTN_SCALE).astype(BF16)
    kb_ref[...] = seg(4)
    vb_ref[...] = seg(5)
    z = (_dot(hb, wf_ref[...]) + bf_ref[...])[:, :lf_ref.shape[-1]]
    lf_ref[...] = _log_sigmoid(z)
    gate_ref[...] = jax.nn.sigmoid(_dot(hb, wg_ref[...]) + bg_ref[...]).astype(BF16)


def _project_sample(x2d, shift, scale, g_pre, tabs, wqkv, wf, wg, bf, bg, n_heads_b):
    n, d = x2d.shape
    tm = min(TOKEN_TILE, n)
    const = lambda a: pl.BlockSpec(a.shape, lambda i: (0, 0), pipeline_mode=pl.Buffered(1))
    row = lambda w: pl.BlockSpec((tm, w), lambda i: (i, 0))
    gw = wg.shape[1]
    return pl.pallas_call(
        _proj_kernel,
        grid=(n // tm,),
        in_specs=[row(d), row(d), row(d), const(g_pre), row(LANES), row(LANES), row(LANES),
                  const(wqkv), const(wf), const(wg), const(bf), const(bg)],
        out_specs=[row(QKV_WIDTH)] * 6 + [row(n_heads_b), row(gw)],
        out_shape=[jax.ShapeDtypeStruct((n, QKV_WIDTH), BF16),
                   jax.ShapeDtypeStruct((n, QKV_WIDTH), F32),
                   jax.ShapeDtypeStruct((n, QKV_WIDTH), F32),
                   jax.ShapeDtypeStruct((n, QKV_WIDTH), BF16),
                   jax.ShapeDtypeStruct((n, QKV_WIDTH), F32),
                   jax.ShapeDtypeStruct((n, QKV_WIDTH), F32),
                   jax.ShapeDtypeStruct((n, n_heads_b), F32),
                   jax.ShapeDtypeStruct((n, gw), BF16)],
        compiler_params=pltpu.CompilerParams(dimension_semantics=("arbitrary",), vmem_limit_bytes=VMEM_LIMIT),
        name="sample_projection",
    )(x2d, shift, scale, g_pre, *tabs, wqkv, wf, wg, bf, bg)


def _proj_prompt_kernel(x_ref, shift_ref, scale_ref, g_ref, cos_ref, sa_ref, sb_ref, cost_ref, sint_ref,
                        wq_ref, wv_ref, wt_ref, wft_ref, wg_ref, bft_ref, bg_ref,
                        qa_ref, kat_ref, va_ref, vab_ref, qb_ref, kbt_ref, vbt_ref, lft_ref, gate_ref):
    hb = _normed_input(x_ref, shift_ref, scale_ref, g_ref)
    hbt = hb.T
    w = QKV_WIDTH
    qa = _rope_lanes(_dot(hb, wq_ref[:, :w]), cos_ref[...], sa_ref[...], sb_ref[...])
    qa_ref[...] = (qa * ATTN_SCALE).astype(BF16)
    qb_ref[...] = (_dot(hb, wq_ref[:, w:]) * ATTN_SCALE).astype(BF16)
    v = _dot(hb, wv_ref[...])
    vab_ref[...] = v.astype(BF16)
    for h in range(va_ref.shape[1]):
        va_ref[:, h, :] = v[:, h * LANES:(h + 1) * LANES]

    kat = _dot(wt_ref[0:w, :], hbt)
    cos_t, sin_t = cost_ref[...], sint_ref[...]
    half = ROT_DIM // 2
    parts = []
    for c in range(w // HEAD_DIM):
        base = c * HEAD_DIM
        x1 = kat[base:base + half]
        x2 = kat[base + half:base + ROT_DIM]
        parts += [x1 * cos_t - x2 * sin_t, x2 * cos_t + x1 * sin_t, kat[base + ROT_DIM:base + HEAD_DIM]]
    kat_ref[0] = jnp.concatenate(parts, axis=0)
    kbt_ref[0] = _dot(wt_ref[w:2 * w, :], hbt)
    vbt_ref[0] = _dot(wt_ref[2 * w:3 * w, :], hbt)
    nh = lft_ref.shape[1]
    lft_ref[0] = _log_sigmoid(_dot(wft_ref[...], hbt)[:nh] + bft_ref[...])
    gate_ref[...] = jax.nn.sigmoid(_dot(hb, wg_ref[...]) + bg_ref[...]).astype(BF16)


def _project_prompt(x2d, shift, scale, g_pre, tabs, tabs_t, wq, wv, wt, wft, wg, bft, bg, bsz, seq, a_heads, b_heads):
    n, d = x2d.shape
    tm = min(TOKEN_TILE, seq)
    tps = seq // tm
    w = QKV_WIDTH
    const = lambda a: pl.BlockSpec(a.shape, lambda i: (0, 0), pipeline_mode=pl.Buffered(1))
    row = lambda width: pl.BlockSpec((tm, width), lambda i: (i, 0))
    mod_spec = pl.BlockSpec((1, 1, d), lambda i: (i // tps, 0, 0))
    tab_spec = pl.BlockSpec((tm, LANES), lambda i: (i % tps, 0))
    tabt_spec = pl.BlockSpec((ROT_DIM // 2, tm), lambda i: (0, i % tps))
    tr_spec = lambda rows: pl.BlockSpec((1, rows, tm), lambda i: (i // tps, 0, i % tps))
    gw = wg.shape[1]
    return pl.pallas_call(
        _proj_prompt_kernel,
        grid=(n // tm,),
        in_specs=[row(d), mod_spec, mod_spec, const(g_pre), tab_spec, tab_spec, tab_spec, tabt_spec, tabt_spec,
                  const(wq), const(wv), const(wt), const(wft), const(wg), const(bft), const(bg)],
        out_specs=[row(w), tr_spec(w), pl.BlockSpec((tm, a_heads, LANES), lambda i: (i, 0, 0)), row(w),
                   row(w), tr_spec(w), tr_spec(w), tr_spec(b_heads), row(gw)],
        out_shape=[jax.ShapeDtypeStruct((n, w), BF16),
                   jax.ShapeDtypeStruct((bsz, w, seq), F32),
                   jax.ShapeDtypeStruct((n, a_heads, LANES), F32),
                   jax.ShapeDtypeStruct((n, w), BF16),
                   jax.ShapeDtypeStruct((n, w), BF16),
                   jax.ShapeDtypeStruct((bsz, w, seq), F32),
                   jax.ShapeDtypeStruct((bsz, w, seq), F32),
                   jax.ShapeDtypeStruct((bsz, b_heads, seq), F32),
                   jax.ShapeDtypeStruct((n, gw), BF16)],
        compiler_params=pltpu.CompilerParams(dimension_semantics=("arbitrary",), vmem_limit_bytes=VMEM_LIMIT),
        name="prompt_projection",
    )(x2d, shift, scale, g_pre, *tabs, *tabs_t, wq, wv, wt, wft, wg, bft, bg)


def _rope_angles(pos):
    half = ROT_DIM // 2
    inv_freq = jnp.power(ROPE_THETA, -jnp.arange(half, dtype=F32) * 2.0 / ROT_DIM)
    return pos[:, None] * inv_freq[None, :]


def _rope_tables(pos):
    half = ROT_DIM // 2
    ang = _rope_angles(pos)
    cos, sin = jnp.cos(ang), jnp.sin(ang)
    s = pos.shape[0]
    ones = jnp.ones((s, HEAD_DIM - ROT_DIM), F32)
    zeros = jnp.zeros((s, HEAD_DIM - ROT_DIM), F32)
    zh = jnp.zeros((s, half), F32)
    cos_t = jnp.concatenate([cos, cos, ones], axis=1)
    sa_t = jnp.concatenate([zh, sin, zeros], axis=1)
    sb_t = jnp.concatenate([-sin, zh, zeros], axis=1)
    rep = LANES // HEAD_DIM
    return tuple(jnp.tile(t, (1, rep)) for t in (cos_t, sa_t, sb_t))


def _cum_kernel(lft_ref, c_ref, ct_ref, *, chunk):
    nh, s = lft_ref.shape[1], lft_ref.shape[2]
    r = lax.broadcasted_iota(jnp.int32, (chunk, chunk), 0)
    c = lax.broadcasted_iota(jnp.int32, (chunk, chunk), 1)
    upto = (r <= c).astype(F32)
    carry = jnp.zeros((nh, 1), F32)
    for i in range(s // chunk):
        x = lft_ref[0, :, i * chunk:(i + 1) * chunk]
        cum = jnp.dot(x, upto, preferred_element_type=F32, precision=lax.Precision.HIGHEST) + carry
        carry = cum[:, chunk - 1:chunk]
        cum_n = cum.T
        for p in range(nh // 2):
            ct_ref[0, p, :, i * chunk:(i + 1) * chunk] = cum[2 * p:2 * p + 2, :]
            c_ref[0, p, i * chunk:(i + 1) * chunk, :] = cum_n[:, 2 * p:2 * p + 2]


def _cum_logf(lft):
    b, nh, s = lft.shape
    chunk = min(256, s)
    return pl.pallas_call(
        functools.partial(_cum_kernel, chunk=chunk),
        grid=(b,),
        in_specs=[pl.BlockSpec((1, nh, s), lambda i: (i, 0, 0))],
        out_specs=[pl.BlockSpec((1, nh // 2, s, 2), lambda i: (i, 0, 0, 0)),
                   pl.BlockSpec((1, nh // 2, 2, s), lambda i: (i, 0, 0, 0))],
        out_shape=[jax.ShapeDtypeStruct((b, nh // 2, s, 2), F32),
                   jax.ShapeDtypeStruct((b, nh // 2, 2, s), F32)],
        name="forget_cumsum",
    )(lft)


def _pattn_kernel(*refs, tq, mode):
    if mode == "diff":
        q_ref, kt_ref, v_ref, lq1, lk1, lq2, lk2, gs_ref, o_ref, kt_s, v_s, m_s, acc_s = refs
    else:
        q_ref, kt_ref, vt_ref, c_ref, ct_ref, o_ref, kt_s, v_s, m_s, acc_s = refs
    qi = pl.program_id(2)
    tk = KV_TILES * tq

    @pl.when(qi == 0)
    def _():
        kt_s[...] = kt_ref[0].astype(BF16)
        if mode == "diff":
            v_s[:, :LANES] = v_ref[0]
            v_s[:, LANES:] = jnp.ones((v_s.shape[0], LANES), BF16)
        else:
            v = vt_ref[0].T
            first = lax.broadcasted_iota(jnp.int32, v.shape, 1) < HEAD_DIM
            v_s[0] = jnp.where(first, v, 1.0).astype(BF16)
            v_s[1] = jnp.where(first, 1.0, v).astype(BF16)

    q = q_ref[0]
    lane = lax.broadcasted_iota(jnp.int32, q.shape, 1)
    zero = jnp.zeros_like(q)
    qz = (jnp.where(lane < HEAD_DIM, q, zero), jnp.where(lane >= HEAD_DIM, q, zero))
    m_s[...] = jnp.full(m_s.shape, -jnp.inf, F32)
    acc_s[...] = jnp.zeros(acc_s.shape, F32)
    if mode == "fox":
        cb = c_ref[0, 0]
        rowb = tuple(jnp.broadcast_to(cb[:, hf:hf + 1], (tq, LANES)) for hf in range(2))
    row = lax.broadcasted_iota(jnp.int32, (tq, LANES), 0)
    col = lax.broadcasted_iota(jnp.int32, (tq, LANES), 1)
    n_acc = acc_s.shape[2] // LANES

    def tile(off, width, masked):
        kt = kt_s[:, pl.ds(off, width)]
        for hf in range(2):
            vt = v_s[pl.ds(off, width), :] if mode == "diff" else v_s[hf, pl.ds(off, width), :]
            s = _dot(qz[hf], kt)
            chunks = []
            for c in range(width // LANES):
                sc = s[:, c * LANES:(c + 1) * LANES]
                if mode == "fox":
                    sc = sc + (rowb[hf] - ct_ref[0, 0, hf:hf + 1, pl.ds(off + c * LANES, LANES)])
                if masked:
                    sc = jnp.where(col + c * LANES <= row, sc, -jnp.inf)
                chunks.append(sc)
            mx = chunks[0]
            for sc in chunks[1:]:
                mx = jnp.maximum(mx, sc)
            m_old = m_s[hf]
            m_new = jnp.maximum(m_old, jnp.max(mx, axis=1, keepdims=True))
            alpha = jnp.exp(m_old - m_new)
            p = jnp.concatenate([jnp.exp(sc - m_new) for sc in chunks], axis=1).astype(BF16)
            acc_s[hf] = jnp.concatenate([alpha] * n_acc, axis=1) * acc_s[hf] + _dot(p, vt)
            m_s[hf] = m_new

    def body(j, carry):
        tile(pl.multiple_of(j * tk, tk), tk, False)
        return carry

    lax.fori_loop(0, qi // KV_TILES, body, 0)
    if KV_TILES == 2:
        @pl.when(qi % 2 == 1)
        def _():
            tile(pl.multiple_of((qi - 1) * tq, tq), tq, False)

    tile(pl.multiple_of(qi * tq, tq), tq, True)

    if mode == "diff":
        a0, a1 = acc_s[0], acc_s[1]
        o = a0[:, :LANES] / a0[:, LANES:] - _lam(lq1, lk1, lq2, lk2) * (a1[:, :LANES] / a1[:, LANES:])
        o = _rms(o, SUBLN_EPS) * gs_ref[...] * (1.0 - LAMBDA_INIT)
    else:
        a0, a1 = acc_s[0], acc_s[1]
        o = jnp.where(lane < HEAD_DIM, a0 / pltpu.roll(a0, HEAD_DIM, 1), a1 / pltpu.roll(a1, HEAD_DIM, 1))
    o_ref[0] = o.astype(BF16)


def _prompt_attention(q, kt, v, extra, mode):
    b, s, w = q.shape
    tq = min(ATTN_TILE, s)
    nblk = w // LANES
    qspec = pl.BlockSpec((1, tq, LANES), lambda bi, h, qi: (bi, qi, h))
    ktspec = pl.BlockSpec((1, LANES, s), lambda bi, h, qi: (bi, h, 0))
    if mode == "diff":
        small = lambda a: pl.BlockSpec(a.shape, lambda bi, h, qi: (0, 0))
        in_specs = [qspec, ktspec, pl.BlockSpec((1, s, LANES), lambda bi, h, qi: (bi, 0, h))] + [small(a) for a in extra]
        scratch = [pltpu.VMEM((LANES, s), BF16), pltpu.VMEM((s, 2 * LANES), BF16),
                   pltpu.VMEM((2, tq, LANES), F32), pltpu.VMEM((2, tq, 2 * LANES), F32)]
    else:
        in_specs = [qspec, ktspec, ktspec,
                    pl.BlockSpec((1, 1, tq, 2), lambda bi, h, qi: (bi, h, qi, 0)),
                    pl.BlockSpec((1, 1, 2, s), lambda bi, h, qi: (bi, h, 0, 0))]
        scratch = [pltpu.VMEM((LANES, s), BF16), pltpu.VMEM((2, s, LANES), BF16),
                   pltpu.VMEM((2, tq, LANES), F32), pltpu.VMEM((2, tq, LANES), F32)]
    return pl.pallas_call(
        functools.partial(_pattn_kernel, tq=tq, mode=mode),
        grid=(b, nblk, s // tq),
        in_specs=in_specs,
        out_specs=qspec,
        out_shape=jax.ShapeDtypeStruct((b, s, w), BF16),
        scratch_shapes=scratch,
        compiler_params=pltpu.CompilerParams(dimension_semantics=("arbitrary", "arbitrary", "arbitrary"),
                                             vmem_limit_bytes=VMEM_LIMIT),
        name="prompt_attention_" + mode,
    )(q, kt, v, *extra)


def _decode_kernel(pt_ref, *refs, n_grp, mode):
    del pt_ref
    q_ref, kn_ref, vn_ref = refs[:3]
    pos = 3
    if mode == "diff":
        lq1, lk1, lq2, lk2, gs_ref = refs[pos:pos + 5]
        pos += 5
    else:
        lfn_ref = refs[pos]
        pos += 1
    kp = refs[pos:pos + n_grp]
    vp = refs[pos + n_grp:pos + 2 * n_grp]
    pos += 2 * n_grp
    if mode == "fox":
        lfp = refs[pos:pos + n_grp]
        pos += n_grp
    o_ref, m_s, l_s, acc_s = refs[pos:pos + 4]
    if mode == "fox":
        carry_s = refs[pos + 4]
    j = pl.program_id(1)
    nq, w = q_ref.shape[1], q_ref.shape[2]
    nchunk = w // HEAD_DIM
    rows = nq * nchunk
    page = kp[0].shape[2]
    rep = w // LANES

    def rep_rows(x):
        return jnp.concatenate([jnp.broadcast_to(x[i:i + 1], (nchunk, x.shape[1])) for i in range(nq)], axis=0)

    rid = lax.broadcasted_iota(jnp.int32, (rows, w), 0)
    lid = lax.broadcasted_iota(jnp.int32, (rows, w), 1)
    qbd = jnp.where(lid // HEAD_DIM == rid % nchunk, rep_rows(q_ref[0].astype(F32)), 0.0)
    qbd_b = qbd.astype(BF16)
    if mode == "fox":
        lfn = lfn_ref[0]
        cn = [lfn[0:1]]
        for i in range(1, nq):
            cn.append(cn[-1] + lfn[i:i + 1])
        r8 = lax.broadcasted_iota(jnp.int32, (rows, nchunk), 0)
        l8 = lax.broadcasted_iota(jnp.int32, (rows, nchunk), 1)
        sel = l8 == r8 % nchunk
        pick = lambda x: jnp.sum(jnp.where(sel, x, 0.0), axis=1, keepdims=True)
        rowb = pick(jnp.concatenate([jnp.broadcast_to(c, (nchunk, nchunk)) for c in cn], axis=0))
        rowb_l = jnp.broadcast_to(rowb, (rows, LANES))

    @pl.when(j == 0)
    def _():
        kn, vn = kn_ref[0], vn_ref[0]
        qq = lax.broadcasted_iota(jnp.int32, (rows, 1), 0) // nchunk
        ss = []
        for jn in range(nq):
            s = jnp.sum(qbd * kn[jn:jn + 1], axis=1, keepdims=True)
            if mode == "fox":
                s = s + (rowb - pick(jnp.broadcast_to(cn[jn], (rows, nchunk))))
            ss.append(jnp.where(jn <= qq, s, -jnp.inf))
        m = ss[0]
        for s in ss[1:]:
            m = jnp.maximum(m, s)
        ps = [jnp.exp(s - m) for s in ss]
        l = ps[0]
        acc = ps[0] * vn[0:1]
        for jn in range(1, nq):
            l = l + ps[jn]
            acc = acc + ps[jn] * vn[jn:jn + 1]
        m_s[...] = jnp.broadcast_to(m, m_s.shape)
        l_s[...] = jnp.broadcast_to(l, l_s.shape)
        acc_s[...] = acc
        if mode == "fox":
            carry_s[...] = jnp.zeros(carry_s.shape, F32)

    ss = [_dot(qbd_b, kp[g][0].astype(BF16)) for g in range(n_grp)]
    if mode == "fox":
        kj = lax.broadcasted_iota(jnp.int32, (page, page), 0)
        ks = lax.broadcasted_iota(jnp.int32, (page, page), 1)
        later = (kj > ks).astype(F32)
        carry = carry_s[...]
        for g in range(n_grp):
            x = lfp[g][0]
            tail = jnp.dot(x, later, preferred_element_type=F32, precision=lax.Precision.HIGHEST) + carry
            carry = carry + jnp.sum(x, axis=1, keepdims=True)
            ss[g] = ss[g] + (rowb_l + jnp.concatenate([tail] * nq, axis=0))
        carry_s[...] = jnp.broadcast_to(carry, carry_s.shape)
    mx = ss[0]
    for s in ss[1:]:
        mx = jnp.maximum(mx, s)
    m_old = m_s[...]
    m_new = jnp.maximum(m_old, jnp.max(mx, axis=1, keepdims=True))
    alpha = jnp.exp(m_old - m_new)
    ps = [jnp.exp(s - m_new) for s in ss]
    lsum = ps[0]
    for p in ps[1:]:
        lsum = lsum + p
    l_s[...] = alpha * l_s[...] + jnp.sum(lsum, axis=1, keepdims=True)
    pv = None
    for g in range(n_grp):
        pb = ps[g].astype(BF16)
        if mode == "diff":
            t = jnp.concatenate([_dot(pb, vp[g][0, pl.ds(h, page, stride=rep), :].astype(BF16))
                                 for h in range(rep)], axis=1)
        else:
            t = lax.dot_general(pb, vp[g][0].astype(BF16), NT_DIMS, preferred_element_type=F32)
        pv = t if pv is None else pv + t
    acc_s[...] = jnp.concatenate([alpha] * rep, axis=1) * acc_s[...] + pv
    m_s[...] = m_new

    @pl.when(j == pl.num_programs(1) - 1)
    def _():
        o = acc_s[...] / jnp.concatenate([l_s[...]] * rep, axis=1)
        c8 = lax.broadcasted_iota(jnp.int32, (nchunk, w), 0)
        l8w = lax.broadcasted_iota(jnp.int32, (nchunk, w), 1)
        if mode == "diff":
            sign = jnp.where(c8 % 2 == 0, 1.0, -_lam(lq1, lk1, lq2, lk2))
            coef = jnp.where(l8w // (2 * HEAD_DIM) == c8 // 2, sign, 0.0)
        else:
            coef = jnp.where(l8w // HEAD_DIM == c8, 1.0, 0.0)
        o4 = jnp.sum(o.reshape(nq, nchunk, w) * coef[None], axis=1)
        if mode == "diff":
            hw = 2 * HEAD_DIM
            o4 = jnp.concatenate(
                [_rms(o4[:, h * hw:(h + 1) * hw], SUBLN_EPS) * gs_ref[...] for h in range(w // hw)],
                axis=1) * (1.0 - LAMBDA_INIT)
        o_ref[0] = o4.astype(BF16)


def _decode_page_map(b, j, pt, *, g, n_grp, n_pages, ndim):
    return (pt[b, n_pages - 1 - (j * n_grp + g)],) + (0,) * (ndim - 1)


def _decode_attention(page_table, q, k_new, v_new, extra, cache_kt, cache_v, cache_lft, mode):
    db, nq, w = q.shape
    n_pages = page_table.shape[1]
    n_grp = min(DECODE_PAGES, n_pages)
    seq = lambda a: pl.BlockSpec((1,) + a.shape[1:], lambda b, j, pt: (b,) + (0,) * (a.ndim - 1))
    small = lambda a: pl.BlockSpec(a.shape, lambda b, j, pt: (0,) * a.ndim)
    pages = lambda a: [pl.BlockSpec((1,) + a.shape[1:],
                                    functools.partial(_decode_page_map, g=g, n_grp=n_grp, n_pages=n_pages, ndim=a.ndim))
                       for g in range(n_grp)]
    in_specs = [seq(q), seq(k_new), seq(v_new)]
    if mode == "diff":
        in_specs += [small(a) for a in extra]
    else:
        in_specs += [seq(extra[0])]
    in_specs += pages(cache_kt) + pages(cache_v)
    args = [q, k_new, v_new, *extra] + [cache_kt] * n_grp + [cache_v] * n_grp
    nchunk = w // HEAD_DIM
    rows = nq * nchunk
    scratch = [pltpu.VMEM((rows, LANES), F32), pltpu.VMEM((rows, LANES), F32), pltpu.VMEM((rows, w), F32)]
    if mode == "fox":
        in_specs += pages(cache_lft)
        args += [cache_lft] * n_grp
        scratch.append(pltpu.VMEM((nchunk, LANES), F32))
    return pl.pallas_call(
        functools.partial(_decode_kernel, n_grp=n_grp, mode=mode),
        grid_spec=pltpu.PrefetchScalarGridSpec(
            num_scalar_prefetch=1,
            grid=(db, n_pages // n_grp),
            in_specs=in_specs,
            out_specs=pl.BlockSpec((1, nq, w), lambda b, j, pt: (b, 0, 0)),
            scratch_shapes=scratch),
        out_shape=jax.ShapeDtypeStruct((db, nq, w), BF16),
        compiler_params=pltpu.CompilerParams(dimension_semantics=("arbitrary", "arbitrary"),
                                             vmem_limit_bytes=VMEM_LIMIT),
        name="sample_attention_" + mode,
    )(page_table, *args)


def _merge_kernel(oa_ref, ob_ref, ga_ref, gb_ref, x_ref, g1_ref, sh2_ref, sc2_ref,
                  wa_ref, wb_ref, wo_ref, gpost_ref, gpre_ref, wr_hi_ref, wr_lo_ref, br_ref, cnt_in_ref,
                  *rest, n_alias):
    x1_ref, h2_ref, pos_ref, w_ref, tcnt_ref, cbefore_ref, cnt_ref, carry_s = rest[n_alias:]
    i = pl.program_id(0)

    @pl.when(i == 0)
    def _():
        carry_s[...] = cnt_in_ref[...]

    d = x_ref.shape[-1]
    y = ga_ref[...] * _dot(oa_ref[...], wa_ref[...]) + gb_ref[...] * _dot(ob_ref[...], wb_ref[...])
    y = _dot(y.astype(BF16), wo_ref[...])
    x1 = x_ref[...] + g1_ref[...].reshape(-1, d) * (_rms(y, NORM_EPS) * gpost_ref[...])
    x1_ref[...] = x1
    h = _rms(x1, NORM_EPS) * gpre_ref[...]
    h = h * (1.0 + sc2_ref[...].reshape(-1, d)) + sh2_ref[...].reshape(-1, d)
    h2_ref[...] = h

    h_hi, h_lo = _split_bf16(h)
    nt = lambda a, b: lax.dot_general(a, b, NT_DIMS, preferred_element_type=F32)
    logit = nt(wr_hi_ref[...], h_hi) + nt(wr_hi_ref[...], h_lo) + nt(wr_lo_ref[...], h_hi) + br_ref[...]
    ne, tm = logit.shape
    e_iota = lax.broadcasted_iota(jnp.int32, (ne, tm), 0)
    idxs, vals = [], []
    for _ in range(TOP_K):
        mx = jnp.max(logit, axis=0, keepdims=True)
        ik = jnp.min(jnp.where(logit == mx, e_iota, ne), axis=0, keepdims=True)
        idxs.append(ik)
        vals.append(mx)
        logit = jnp.where(e_iota == ik, -jnp.inf, logit)
    ex = [jnp.exp(v - vals[0]) for v in vals]
    den = ex[0]
    for e in ex[1:]:
        den = den + e
    onehot = jnp.zeros((ne, tm), F32)
    for ik in idxs:
        onehot = onehot + (e_iota == ik).astype(F32)
    onehot_b = onehot.astype(BF16)
    tr = lax.broadcasted_iota(jnp.int32, (tm, tm), 0)
    tc = lax.broadcasted_iota(jnp.int32, (tm, tm), 1)
    er = lax.broadcasted_iota(jnp.int32, (ne, ne), 0)
    ec = lax.broadcasted_iota(jnp.int32, (ne, ne), 1)
    tile_cnt = jnp.ceil(jnp.sum(onehot, axis=1, keepdims=True) * (1.0 / RUN_ALIGN)) * RUN_ALIGN
    lower = _dot((ec < er).astype(BF16), jnp.broadcast_to(tile_cnt, (ne, LANES)).astype(BF16))[:, 0:1]
    slot = _dot(onehot_b, (tr < tc).astype(BF16)) + lower
    for k in range(TOP_K):
        w_ref[k:k + 1, :] = ex[k] / den
        pos_ref[k:k + 1, :] = jnp.sum(jnp.where(e_iota == idxs[k], slot, 0.0), axis=0, keepdims=True).astype(jnp.int32)
    tcnt_ref[0] = tile_cnt
    cbefore_ref[0] = carry_s[...]
    carry_s[...] = carry_s[...] + tile_cnt
    cnt_ref[...] = carry_s[...]


def _merge_route(oa, ob, gates, x2d, g1, sh2, sc2, wa, wb, wo, gpost, gpre, wr_hi, wr_lo, br, cnt_in,
                 n_total, row_offset, prev, rows_per_seq, tm):
    n, d = x2d.shape
    off = row_offset // tm
    nt_total = n_total // tm
    if g1.ndim == 3:
        tps = rows_per_seq // tm
        mod_spec = pl.BlockSpec((1, 1, d), lambda i: (i // tps, 0, 0))
    else:
        mod_spec = pl.BlockSpec((tm, d), lambda i: (i, 0))
    const = lambda a: pl.BlockSpec(a.shape, lambda i: (0, 0), pipeline_mode=pl.Buffered(1))
    row = lambda w, c=0: pl.BlockSpec((tm, w), lambda i: (i, c))
    anyspec = pl.BlockSpec(memory_space=pl.ANY)
    n_alias = len(prev)
    per_tile = jax.ShapeDtypeStruct((nt_total, N_EXPERTS, 1), F32)
    per_tile_spec = pl.BlockSpec((1, N_EXPERTS, 1), lambda i: (i + off, 0, 0))
    out_shape = [jax.ShapeDtypeStruct((n_total, d), F32), jax.ShapeDtypeStruct((n_total, d), F32),
                 jax.ShapeDtypeStruct((TOP_K, n_total), jnp.int32), jax.ShapeDtypeStruct((TOP_K, n_total), F32),
                 per_tile, per_tile, jax.ShapeDtypeStruct((N_EXPERTS, 1), F32)]
    out_specs = [pl.BlockSpec((tm, d), lambda i: (i + off, 0)), pl.BlockSpec((tm, d), lambda i: (i + off, 0)),
                 pl.BlockSpec((TOP_K, tm), lambda i: (0, i + off)), pl.BlockSpec((TOP_K, tm), lambda i: (0, i + off)),
                 per_tile_spec, per_tile_spec, pl.BlockSpec((N_EXPERTS, 1), lambda i: (0, 0))]
    n_in = 17
    return pl.pallas_call(
        functools.partial(_merge_kernel, n_alias=n_alias),
        grid=(n // tm,),
        in_specs=[row(oa.shape[1]), row(ob.shape[1]), row(d, 0), row(d, 1), row(d), mod_spec, mod_spec, mod_spec,
                  const(wa), const(wb), const(wo), const(gpost), const(gpre), const(wr_hi), const(wr_lo), const(br),
                  const(cnt_in)] + [anyspec] * n_alias,
        out_specs=out_specs,
        out_shape=out_shape,
        input_output_aliases={n_in + a: a for a in range(n_alias)},
        scratch_shapes=[pltpu.VMEM((N_EXPERTS, 1), F32)],
        compiler_params=pltpu.CompilerParams(dimension_semantics=("arbitrary",), vmem_limit_bytes=VMEM_LIMIT),
        name="merge_and_route",
    )(oa, ob, gates, gates, x2d, g1, sh2, sc2, wa, wb, wo, gpost, gpre, wr_hi, wr_lo, br, cnt_in, *prev)


def _tile_rows(tm):
    return TOP_K * tm + RUN_ALIGN * N_EXPERTS


def _run_copies(cnt_ref, dst_ref, tile, tm, make_copy, wait=False):
    def body(e, src):
        count = cnt_ref[tile * N_EXPERTS + e]
        dst = dst_ref[tile * N_EXPERTS + e]
        size = tm
        lane = 0
        while size >= RUN_ALIGN:
            above = (count // (2 * size)) * (2 * size)

            @pl.when((count & size) != 0)
            def _(above=above, size=size, lane=lane):
                copy = make_copy(pl.multiple_of(src + above, RUN_ALIGN), pl.multiple_of(dst + above, RUN_ALIGN), size)
                if wait:
                    copy.wait()
                else:
                    copy.start(priority=lane % 2)

            size //= 2
            lane += 1
        return src + count

    lax.fori_loop(0, N_EXPERTS, body, 0)


def _dispatch_kernel(cnt_ref, dst_ref, pos_ref, h_ref, xd_ref, xs_s, sem):
    i = pl.program_id(0)
    n_tiles = pl.num_programs(0)
    slot = i % 2
    tm = h_ref.shape[0]
    rows = xs_s.shape[1]

    def copies(tile, of_slot, wait):
        _run_copies(cnt_ref, dst_ref, tile, tm,
                    lambda src, dst, size: pltpu.make_async_copy(
                        xs_s.at[of_slot, pl.ds(src, size), :], xd_ref.at[pl.ds(dst, size), :], sem.at[of_slot]),
                    wait=wait)

    @pl.when(i >= 2)
    def _():
        copies(i - 2, slot, True)

    pos = pos_ref[...]
    p_iota = lax.broadcasted_iota(jnp.int32, (rows, tm), 0)
    sel = p_iota == pos[0:1]
    for k in range(1, TOP_K):
        sel = sel | (p_iota == pos[k:k + 1])
    xs_s[slot] = _dot(jnp.where(sel, 1.0, 0.0).astype(BF16), h_ref[...].astype(BF16))
    copies(i, slot, False)

    @pl.when(i == n_tiles - 1)
    def _():
        copies(i, slot, True)

        @pl.when(n_tiles > 1)
        def _():
            copies(i - 1, 1 - slot, True)


def _dispatch(tile_cnt, tile_dst, pos, h2, n_slots, tm):
    n, d = h2.shape
    return pl.pallas_call(
        _dispatch_kernel,
        grid_spec=pltpu.PrefetchScalarGridSpec(
            num_scalar_prefetch=2,
            grid=(n // tm,),
            in_specs=[pl.BlockSpec((TOP_K, tm), lambda i, c, s: (0, i)),
                      pl.BlockSpec((tm, d), lambda i, c, s: (i, 0))],
            out_specs=pl.BlockSpec(memory_space=pl.ANY),
            scratch_shapes=[pltpu.VMEM((2, _tile_rows(tm), d), F32), pltpu.SemaphoreType.DMA((2,))]),
        out_shape=jax.ShapeDtypeStruct((n_slots, d), F32),
        compiler_params=pltpu.CompilerParams(dimension_semantics=("arbitrary",), vmem_limit_bytes=VMEM_LIMIT),
        name="moe_dispatch",
    )(tile_cnt, tile_dst, pos, h2)


def _expert_kernel(be_ref, nv_ref, nu_ref, x_ref, w1_ref, b1_ref, w2_ref, b2_ref, y_ref, w1_s, w2_s):
    i = pl.program_id(0)

    @pl.when(i < nu_ref[0])
    def _():
        @pl.when((i == 0) | (be_ref[i] != be_ref[jnp.maximum(i - 1, 0)]))
        def _():
            w1_s[...] = w1_ref[0].astype(BF16)
            w2_s[...] = w2_ref[0].astype(BF16)

        x = x_ref[...]
        row = lax.broadcasted_iota(jnp.int32, (x.shape[0], 1), 0)
        x = jnp.where(row < nv_ref[i], x, 0.0).astype(BF16)
        hh = _dot(x, w1_s[...]) + b1_ref[0]
        f = hh.shape[1] // 2
        x_glu = jnp.minimum(hh[:, :f], SWIGLU_LIMIT)
        x_lin = jnp.clip(hh[:, f:], -SWIGLU_LIMIT, SWIGLU_LIMIT)
        act = x_glu * jax.nn.sigmoid(SWIGLU_ALPHA * x_glu) * (x_lin + 1.0)
        y_ref[...] = _dot(act.astype(BF16), w2_s[...]) + b2_ref[0]


def _experts(block_expert, n_valid, n_used, x_disp, w1, b1, w2, b2):
    n_slots, d = x_disp.shape
    tb = EXPERT_TILE
    ne, _, f2 = w1.shape
    blk = lambda i, be, nv, nu: (jnp.minimum(i, nu[0] - 1), 0)
    wsel = lambda i, be, nv, nu: (be[i], 0, 0)
    return pl.pallas_call(
        _expert_kernel,
        grid_spec=pltpu.PrefetchScalarGridSpec(
            num_scalar_prefetch=3,
            grid=(n_slots // tb,),
            in_specs=[pl.BlockSpec((tb, d), blk),
                      pl.BlockSpec((1, d, f2), wsel), pl.BlockSpec((1, 1, f2), wsel),
                      pl.BlockSpec((1, f2 // 2, d), wsel), pl.BlockSpec((1, 1, d), wsel)],
            out_specs=pl.BlockSpec((tb, d), blk),
            scratch_shapes=[pltpu.VMEM((d, f2), BF16), pltpu.VMEM((f2 // 2, d), BF16)]),
        out_shape=jax.ShapeDtypeStruct((n_slots, d), F32),
        compiler_params=pltpu.CompilerParams(dimension_semantics=("arbitrary",), vmem_limit_bytes=VMEM_LIMIT),
        name="moe_experts",
    )(block_expert, n_valid, n_used, x_disp, w1, b1.reshape(ne, 1, f2), w2, b2.reshape(ne, 1, d))


def _combine_kernel(cnt_ref, dst_ref, pos_ref, w_ref, x1_ref, g2_ref, gpost_ref, yd_ref, o_ref, ys_s, sem, *, tile_off):
    i = pl.program_id(0)
    n_local = pl.num_programs(0)
    slot = i % 2
    tm, d = x1_ref.shape
    rows = ys_s.shape[1]

    def gather(tile, to_slot, wait=False):
        _run_copies(cnt_ref, dst_ref, tile, tm,
                    lambda src, dst, size: pltpu.make_async_copy(
                        yd_ref.at[pl.ds(dst, size), :], ys_s.at[to_slot, pl.ds(src, size), :], sem.at[to_slot]),
                    wait=wait)

    @pl.when(i == 0)
    def _():
        ys_s[...] = jnp.zeros(ys_s.shape, F32)
        gather(tile_off, 0)

    @pl.when(i + 1 < n_local)
    def _():
        gather(tile_off + i + 1, 1 - slot)

    gather(tile_off + i, slot, wait=True)
    pos_t = pos_ref[...].astype(F32).T.astype(jnp.int32)
    w_t = w_ref[...].T
    p_iota = lax.broadcasted_iota(jnp.int32, (tm, rows), 1)
    sel = jnp.zeros((tm, rows), F32)
    for k in range(TOP_K):
        sel = sel + jnp.where(p_iota == pos_t[:, k:k + 1], w_t[:, k:k + 1], 0.0)
    y = _dot(sel.astype(BF16), ys_s[slot].astype(BF16))
    o_ref[...] = x1_ref[...] + g2_ref[...].reshape(-1, d) * (_rms(y, NORM_EPS) * gpost_ref[...])


def _combine(tile_cnt, tile_dst, pos, w, x1, g2, gpost, y_disp, n, row_offset, rows_per_seq, tm):
    d = x1.shape[1]
    off = row_offset // tm
    if g2.ndim == 3:
        tps = rows_per_seq // tm
        mod_spec = pl.BlockSpec((1, 1, d), lambda i, c, s: (i // tps, 0, 0))
    else:
        mod_spec = pl.BlockSpec((tm, d), lambda i, c, s: (i, 0))
    return pl.pallas_call(
        functools.partial(_combine_kernel, tile_off=off),
        grid_spec=pltpu.PrefetchScalarGridSpec(
            num_scalar_prefetch=2,
            grid=(n // tm,),
            in_specs=[pl.BlockSpec((TOP_K, tm), lambda i, c, s: (0, i + off)),
                      pl.BlockSpec((TOP_K, tm), lambda i, c, s: (0, i + off)),
                      pl.BlockSpec((tm, d), lambda i, c, s: (i + off, 0)),
                      mod_spec,
                      pl.BlockSpec((1, d), lambda i, c, s: (0, 0)),
                      pl.BlockSpec(memory_space=pl.ANY)],
            out_specs=pl.BlockSpec((tm, d), lambda i, c, s: (i, 0)),
            scratch_shapes=[pltpu.VMEM((2, _tile_rows(tm), d), F32), pltpu.SemaphoreType.DMA((2,))]),
        out_shape=jax.ShapeDtypeStruct((n, d), F32),
        compiler_params=pltpu.CompilerParams(dimension_semantics=("arbitrary",), vmem_limit_bytes=VMEM_LIMIT),
        name="moe_combine",
    )(tile_cnt, tile_dst, pos, w, x1, g2, gpost, y_disp)


def kernel(x_prompt, x_sample, c_prompt, c_sample, cache_a_k, cache_a_v, cache_b_k, cache_b_v, cache_b_logf, page_table, w_mod, b_mod, g_pre_mix, g_post_mix, g_pre_ffn, g_post_ffn, w_in, b_forget, b_gate, lambda_q1, lambda_k1, lambda_q2, lambda_k2, g_subln, w_br_a, w_br_b, w_out, w_router, b_router, w_mlp1, b_mlp1, w_mlp2, b_mlp2):
    bsz, seq, d = x_prompt.shape
    db, nq, _ = x_sample.shape
    n_pool, page = cache_a_k.shape[1], cache_a_k.shape[2]
    n_pages = page_table.shape[1]
    past_len = n_pages * page
    a_heads = cache_a_k.shape[3]
    b_heads = cache_b_k.shape[3]
    n_p, n_s = bsz * seq, db * nq
    n_tot = n_p + n_s
    w = QKV_WIDTH

    w_in0 = w_in[0]
    wqkv = w_in0[:, :6 * w].astype(BF16)
    wf_cols = w_in0[:, 6 * w:6 * w + b_heads]
    wf = jnp.pad(wf_cols, ((0, 0), (0, LANES - b_heads))).astype(BF16)
    wg = w_in0[:, 6 * w + b_heads:].astype(BF16)
    bf = jnp.pad(b_forget[0], (0, LANES - b_heads)).reshape(1, LANES)
    bg = b_gate[0].reshape(1, -1)
    wq = jnp.concatenate([wqkv[:, 0:w], wqkv[:, 3 * w:4 * w]], axis=1)
    wv = wqkv[:, 2 * w:3 * w]
    wt = jnp.concatenate([wqkv[:, w:2 * w], wqkv[:, 4 * w:5 * w], wqkv[:, 5 * w:6 * w]], axis=1).T
    wft = jnp.pad(wf_cols.T, ((0, 16 - b_heads), (0, 0))).astype(BF16)
    bft = b_forget[0].reshape(b_heads, 1)
    row = lambda a: a[0].reshape(1, -1)
    lam_args = (row(lambda_q1), row(lambda_k1), row(lambda_q2), row(lambda_k2), row(g_subln))
    wr_t = w_router[0].T
    wr_hi = wr_t.astype(BF16)
    wr_lo = (wr_t - wr_hi.astype(F32)).astype(BF16)
    br = b_router[0].reshape(N_EXPERTS, 1)

    mod = _adaln(jnp.concatenate([c_prompt, c_sample], axis=0), w_mod[0], b_mod[0])
    mod_p = [mod[i, :bsz].reshape(bsz, 1, d) for i in range(6)]
    mod_s = [jnp.repeat(mod[i, bsz:], nq, axis=0) for i in range(6)]

    pos_p = jnp.arange(seq, dtype=F32)
    ang_t = _rope_angles(pos_p).T
    tabs_s = tuple(jnp.tile(t, (db, 1)) for t in _rope_tables(past_len + jnp.arange(nq, dtype=F32)))
    xp2 = x_prompt.reshape(n_p, d)
    xs2 = x_sample.reshape(n_s, d)
    qa, kat, va, vab, qb, kbt, vbt, lft, gates = _project_prompt(
        xp2, mod_p[0], mod_p[1], row(g_pre_mix), _rope_tables(pos_p), (jnp.cos(ang_t), jnp.sin(ang_t)),
        wq, wv, wt, wft, wg, bft, bg, bsz, seq, a_heads, b_heads)
    qa_s, ka_s, va_s, qb_s, kb_s, vb_s, lf_s, gates_s = _project_sample(
        xs2, mod_s[0], mod_s[1], row(g_pre_mix), tabs_s, wqkv, wf, wg, bf, bg, b_heads)

    sh3 = lambda a: a.reshape(bsz, seq, w)
    oa = _prompt_attention(sh3(qa), kat, sh3(vab), lam_args, "diff")
    ob = _prompt_attention(sh3(qb), kbt, vbt, _cum_logf(lft), "fox")

    sd3 = lambda a: a.reshape(db, nq, -1)
    kt_a = jnp.transpose(cache_a_k[0], (0, 2, 3, 4, 1)).reshape(n_pool, w, page)
    kt_b = jnp.transpose(cache_b_k[0], (0, 2, 3, 1)).reshape(n_pool, w, page)
    vt_b = jnp.transpose(cache_b_v[0], (0, 2, 3, 1)).reshape(n_pool, w, page)
    lft_b = jnp.transpose(cache_b_logf[0], (0, 2, 1))
    oa_s = _decode_attention(page_table, sd3(qa_s), sd3(ka_s), sd3(va_s), lam_args,
                             kt_a, cache_a_v[0].reshape(n_pool, page * a_heads, 2 * HEAD_DIM), None, "diff")
    ob_s = _decode_attention(page_table, sd3(qb_s), sd3(kb_s), sd3(vb_s), (sd3(lf_s),),
                             kt_b, vt_b, lft_b, "fox")

    wa, wb, wo = w_br_a[0].astype(BF16), w_br_b[0].astype(BF16), w_out[0].astype(BF16)
    common = (wa, wb, wo, row(g_post_mix), row(g_pre_ffn), wr_hi, wr_lo, br)
    tm = math.gcd(TOKEN_TILE, n_s)
    outs_p = _merge_route(oa.reshape(n_p, w), ob.reshape(n_p, w), gates, xp2, mod_p[2], mod_p[3], mod_p[4], *common,
                          jnp.zeros((N_EXPERTS, 1), F32), n_tot, 0, (), seq, tm)
    x1, h2, pos, wts, tile_cnt, tile_before, counts = _merge_route(
        oa_s.reshape(n_s, w), ob_s.reshape(n_s, w), gates_s, xs2, mod_s[2], mod_s[3], mod_s[4], *common,
        outs_p[6], n_tot, n_p, tuple(outs_p[:6]), nq, tm)

    tb = EXPERT_TILE
    max_rows = n_tot * TOP_K + (RUN_ALIGN - 1) * N_EXPERTS * (n_tot // tm)
    n_blocks = (max_rows + N_EXPERTS * (tb - 1)) // tb
    cnt = counts[:, 0].astype(jnp.int32)
    padded = (cnt + tb - 1) // tb * tb
    pad_end = jnp.cumsum(padded)
    pad_start = pad_end - padded
    blk_start = jnp.arange(n_blocks, dtype=jnp.int32) * tb
    block_expert = jnp.minimum(jnp.sum((blk_start[:, None] >= pad_end[None, :]).astype(jnp.int32), axis=1),
                               N_EXPERTS - 1)
    n_valid = jnp.clip(cnt[block_expert] - (blk_start - pad_start[block_expert]), 0, tb).astype(jnp.int32)
    n_used = (pad_end[-1:] // tb).astype(jnp.int32)
    run_cnt = tile_cnt.reshape(-1).astype(jnp.int32)
    run_dst = (tile_before.reshape(-1, N_EXPERTS).astype(jnp.int32) + pad_start[None, :]).reshape(-1)

    x_disp = _dispatch(run_cnt, run_dst, pos, h2, n_blocks * tb, tm)
    y_disp = _experts(block_expert, n_valid, n_used, x_disp,
                      w_mlp1[0], b_mlp1[0], w_mlp2[0], b_mlp2[0])
    y_p = _combine(run_cnt, run_dst, pos, wts, x1, mod_p[5], row(g_post_ffn), y_disp, n_p, 0, seq, tm)
    y_s = _combine(run_cnt, run_dst, pos, wts, x1, mod_s[5], row(g_post_ffn), y_disp, n_s, n_p, nq, tm)

    new_ka = jnp.transpose(kat.reshape(1, bsz, a_heads, 2, HEAD_DIM, seq), (0, 1, 5, 2, 3, 4))
    new_kb = jnp.transpose(kbt.reshape(1, bsz, b_heads, HEAD_DIM, seq), (0, 1, 4, 2, 3))
    new_vb = jnp.transpose(vbt.reshape(1, bsz, b_heads, HEAD_DIM, seq), (0, 1, 4, 2, 3))
    new_lf = jnp.transpose(lft.reshape(1, bsz, b_heads, seq), (0, 1, 3, 2))
    return (y_p.reshape(bsz, seq, d), y_s.reshape(db, nq, d),
            new_ka, va.reshape(1, bsz, seq, a_heads, 2 * HEAD_DIM), new_kb, new_vb, new_lf,
            ka_s.reshape(1, db, nq, a_heads, 2, HEAD_DIM), va_s.reshape(1, db, nq, a_heads, 2 * HEAD_DIM),
            kb_s.reshape(1, db, nq, b_heads, HEAD_DIM), vb_s.reshape(1, db, nq, b_heads, HEAD_DIM),
            lf_s.reshape(1, db, nq, b_heads))
```
